```python
import math
import jax, jax.numpy as jnp
from jax import lax
import numpy as np

D_MODEL = 1024
BATCH = 8
SEQ = 4096
DEPTH = 1
DEC_BATCH = 16
DEC_SEQ = 16
PAST_LEN = 4096

CHUNK = 64
Q_BLOCK = 128
A_HEADS = 4
A_DH = D_MODEL // 16
A_DV = 2 * A_DH
A_WIDTH = A_HEADS * A_DV
G_HEADS = 4
G_DK = D_MODEL // 16
G_DV = D_MODEL // 8
G_WIDTH = G_HEADS * G_DV
G_RANK = 16
G_TAU = 16.0
MIX_WIDTH = A_WIDTH + G_WIDTH
SPLIT_SIZES = (A_HEADS * 2 * A_DH, A_HEADS * 2 * A_DH, A_WIDTH, G_HEADS * G_DK, G_HEADS * G_DK, G_WIDTH, G_RANK, G_WIDTH)
IN_WIDTH = 3088
N_GROUPS = 4
EXPERTS_PER_GROUP = 8
N_EXPERTS = N_GROUPS * EXPERTS_PER_GROUP
TOP_K_INNER = 2
D_EXPERT = D_MODEL // 4
EPS = 1e-6

kernel_name = 'hymba_diffattn_gla_hmoe_stream_step'


def rms_norm(x, gain):
    xf = x.astype(jnp.float32)
    y = xf * lax.rsqrt(jnp.mean(xf * xf, axis=-1, keepdims=True) + EPS)
    return (y * gain.astype(jnp.float32)).astype(x.dtype)


def diff_lambda(lq1, lk1, lq2, lk2, lam_init):
    f32 = jnp.float32
    return (jnp.exp(jnp.sum(lq1.astype(f32) * lk1.astype(f32)))
            - jnp.exp(jnp.sum(lq2.astype(f32) * lk2.astype(f32))) + lam_init)


def diff_attend(q, k, v, lam, mask):
    s = jnp.einsum('bqhmd,bkhmd->bhmqk', q, k, preferred_element_type=jnp.float32) * (A_DH ** -0.5)
    if mask is not None:
        s = jnp.where(mask, s, -jnp.inf)
    p = jax.nn.softmax(s, axis=-1)
    p = p[:, :, 0] - lam * p[:, :, 1]
    return jnp.einsum('bhqk,bkhv->bqhv', p.astype(v.dtype), v)


def diff_attention_prompt(q, k, v, lam):
    B, L = q.shape[0], q.shape[1]
    nqb = L // Q_BLOCK
    q_blocks = jnp.moveaxis(q.reshape(B, nqb, Q_BLOCK, A_HEADS, 2, A_DH), 1, 0)
    key_chunk = jnp.arange(L) // CHUNK

    def one_block(args):
        qb, i = args
        q_chunk = (i * Q_BLOCK + jnp.arange(Q_BLOCK)) // CHUNK
        mask = key_chunk[None, :] <= q_chunk[:, None]
        return diff_attend(qb, k, v, lam, mask)

    o = lax.map(one_block, (q_blocks, jnp.arange(nqb)))
    return jnp.moveaxis(o, 0, 1).reshape(B, L, A_HEADS, A_DV)


def gla_chunk(S, q, k, v, la):
    C = q.shape[1]
    b = jnp.cumsum(la, axis=1)
    causal = jnp.tril(jnp.ones((C, C), dtype=bool))[None, :, :, None, None]
    decay = jnp.exp(jnp.where(causal, b[:, :, None] - b[:, None, :], -jnp.inf))
    scores = jnp.einsum('bthd,btshd,bshd->bhts', q, decay, k)
    o = (jnp.einsum('bhts,bshv->bthv', scores, v)
         + jnp.einsum('bthd,bhdv->bthv', q * jnp.exp(b), S))
    b_last = b[:, -1]
    S_new = (jnp.exp(b_last)[..., None] * S
             + jnp.einsum('bshd,bshv->bhdv', k * jnp.exp(b_last[:, None] - b), v))
    return S_new, o


def gla_prompt(q, k, v, la):
    B, L = q.shape[0], q.shape[1]
    n_chunks = L // CHUNK

    def to_chunks(t):
        return jnp.moveaxis(t.reshape((B, n_chunks, CHUNK) + t.shape[2:]), 1, 0)

    S0 = jnp.zeros((B, G_HEADS, G_DK, G_DV), jnp.float32)
    S, o = lax.scan(lambda S, xs: gla_chunk(S, *xs), S0,
                    (to_chunks(q), to_chunks(k), to_chunks(v), to_chunks(la)))
    return S, jnp.moveaxis(o, 0, 1).reshape(B, L, G_HEADS, G_DV)


def token_mixer(h, w_in, q_gain, k_gain, lam, lam_init, a_out_gain, w_a2, b_a,
                g_out_gain, w_out, cache_k=None, cache_v=None, state=None):
    B, L, _ = h.shape
    f32 = jnp.float32
    z = jnp.einsum('bld,de->ble', h, w_in)
    split_idx = [int(i) for i in np.cumsum(SPLIT_SIZES)[:-1]]
    aq, ak, av, gq, gk, gv, ga, gr = jnp.split(z, split_idx, axis=-1)
    aq = rms_norm(aq.reshape(B, L, A_HEADS, 2, A_DH), q_gain)
    ak = rms_norm(ak.reshape(B, L, A_HEADS, 2, A_DH), k_gain)
    av = av.reshape(B, L, A_HEADS, A_DV)
    if cache_k is None:
        ao = diff_attention_prompt(aq, ak, av, lam)
    else:
        keys = jnp.concatenate([cache_k.astype(ak.dtype), ak], axis=1)
        vals = jnp.concatenate([cache_v.astype(av.dtype), av], axis=1)
        ao = diff_attend(aq, keys, vals, lam, None)
    ao = (rms_norm(ao, a_out_gain) * (1.0 - lam_init)).reshape(B, L, A_WIDTH)
    gq = gq.reshape(B, L, G_HEADS, G_DK).astype(f32) * (G_DK ** -0.5)
    gk = gk.reshape(B, L, G_HEADS, G_DK).astype(f32)
    gv = gv.reshape(B, L, G_HEADS, G_DV).astype(f32)
    la = (jax.nn.log_sigmoid(jnp.einsum('blr,rk->blk', ga, w_a2).astype(f32) + b_a.astype(f32))
          / G_TAU).reshape(B, L, G_HEADS, G_DK)
    if state is None:
        s_new, go = gla_prompt(gq, gk, gv, la)
    else:
        s_new, go = gla_chunk(state.astype(f32), gq, gk, gv, la)
    go = rms_norm(go.astype(h.dtype), g_out_gain).reshape(B, L, G_WIDTH) * jax.nn.silu(gr)
    out = jnp.einsum('ble,ed->bld', jnp.concatenate([ao, go], axis=-1), w_out)
    return out, ak, av, s_new.astype(h.dtype)


def hier_moe(h, w_group, b_group, w_erouter, b_erouter, w_gate, w_up, w_down):
    B, L, D = h.shape
    hf = h.reshape(B * L, D)
    N = hf.shape[0]
    f32 = jnp.float32
    g_logits = (hf @ w_group + b_group).astype(f32)
    g_prob = jax.nn.softmax(g_logits, axis=-1)
    _, g_idx = lax.top_k(g_logits, 1)
    g_w = jnp.take_along_axis(g_prob, g_idx, axis=-1)
    e_logits = (hf @ w_erouter + b_erouter).astype(f32).reshape(N, N_GROUPS, EXPERTS_PER_GROUP)
    e_sel = jnp.take_along_axis(e_logits, g_idx[:, :, None], axis=1)[:, 0]
    top_v, top_i = lax.top_k(e_sel, TOP_K_INNER)
    weights = g_w * jax.nn.softmax(top_v, axis=-1)
    expert_id = g_idx * EXPERTS_PER_GROUP + top_i
    combine = jnp.sum(jax.nn.one_hot(expert_id, N_EXPERTS, dtype=f32) * weights[..., None], axis=1)
    combine = combine.astype(h.dtype)
    y = jnp.zeros_like(hf)
    for e in range(N_EXPERTS):
        ye = (jax.nn.silu(hf @ w_gate[e]) * (hf @ w_up[e])) @ w_down[e]
        y = y + combine[:, e:e + 1] * ye
    return y.reshape(B, L, D)


def setup_inputs(seed: int = 0) -> dict:
    key = jax.random.key(seed)
    ks = jax.random.split(key, 26)
    f32 = jnp.float32

    def nrm(k, shape, scale):
        return jax.random.normal(k, shape, f32) * scale

    def gain(k, shape):
        return 1.0 + 0.02 * jax.random.normal(k, shape, f32)

    return {
        'x_prompt': nrm(ks[0], (BATCH, SEQ, D_MODEL), 1.0),
        'x_sample': nrm(ks[1], (DEC_BATCH, DEC_SEQ, D_MODEL), 1.0),
        'cache_k': nrm(ks[2], (DEPTH, DEC_BATCH, PAST_LEN, A_HEADS, 2, A_DH), 1.0),
        'cache_v': nrm(ks[3], (DEPTH, DEC_BATCH, PAST_LEN, A_HEADS, A_DV), 1.0),
        'state_gla': nrm(ks[4], (DEPTH, DEC_BATCH, G_HEADS, G_DK, G_DV), 1.0),
        'norm1_gain': gain(ks[5], (DEPTH, D_MODEL)),
        'w_in': nrm(ks[6], (DEPTH, D_MODEL, IN_WIDTH), D_MODEL ** -0.5),
        'a_q_gain': gain(ks[7], (DEPTH, A_DH)),
        'a_k_gain': gain(ks[8], (DEPTH, A_DH)),
        'lambda_q1': nrm(ks[9], (DEPTH, A_DH), 0.1),
        'lambda_k1': nrm(ks[10], (DEPTH, A_DH), 0.1),
        'lambda_q2': nrm(ks[11], (DEPTH, A_DH), 0.1),
        'lambda_k2': nrm(ks[12], (DEPTH, A_DH), 0.1),
        'a_out_gain': gain(ks[13], (DEPTH, A_DV)),
        'w_a2': nrm(ks[14], (DEPTH, G_RANK, G_HEADS * G_DK), G_RANK ** -0.5),
        'b_a': nrm(ks[15], (DEPTH, G_HEADS * G_DK), 0.1),
        'g_out_gain': gain(ks[16], (DEPTH, G_DV)),
        'w_out': nrm(ks[17], (DEPTH, MIX_WIDTH, D_MODEL), MIX_WIDTH ** -0.5),
        'norm2_gain': gain(ks[18], (DEPTH, D_MODEL)),
        'w_group': nrm(ks[19], (DEPTH, D_MODEL, N_GROUPS), D_MODEL ** -0.5),
        'b_group': nrm(ks[20], (DEPTH, N_GROUPS), 0.01),
        'w_erouter': nrm(ks[21], (DEPTH, D_MODEL, N_EXPERTS), D_MODEL ** -0.5),
        'b_erouter': nrm(ks[22], (DEPTH, N_EXPERTS), 0.01),
        'w_gate': nrm(ks[23], (DEPTH, N_EXPERTS, D_MODEL, D_EXPERT), D_MODEL ** -0.5),
        'w_up': nrm(ks[24], (DEPTH, N_EXPERTS, D_MODEL, D_EXPERT), D_MODEL ** -0.5),
        'w_down': nrm(ks[25], (DEPTH, N_EXPERTS, D_EXPERT, D_MODEL), D_EXPERT ** -0.5),
    }


def reference(x_prompt, x_sample, cache_k, cache_v, state_gla, norm1_gain, w_in, a_q_gain,
              a_k_gain, lambda_q1, lambda_k1, lambda_q2, lambda_k2, a_out_gain, w_a2, b_a,
              g_out_gain, w_out, norm2_gain, w_group, b_group, w_erouter, b_erouter,
              w_gate, w_up, w_down):
    y_p = x_prompt
    y_s = x_sample
    kp_l, vp_l, sp_l, ks_l, vs_l, ss_l = [], [], [], [], [], []
    for l in range(DEPTH):
        lam_init = 0.8 - 0.6 * math.exp(-0.3 * l)
        lam = diff_lambda(lambda_q1[l], lambda_k1[l], lambda_q2[l], lambda_k2[l], lam_init)
        mix_w = (w_in[l], a_q_gain[l], a_k_gain[l], lam, lam_init, a_out_gain[l], w_a2[l], b_a[l],
                 g_out_gain[l], w_out[l])
        moe_w = (w_group[l], b_group[l], w_erouter[l], b_erouter[l], w_gate[l], w_up[l], w_down[l])
        mix, kp, vp, sp = token_mixer(rms_norm(y_p, norm1_gain[l]), *mix_w)
        y_p = y_p + mix
        y_p = y_p + hier_moe(rms_norm(y_p, norm2_gain[l]), *moe_w)
        mix, kn, vn, sn = token_mixer(rms_norm(y_s, norm1_gain[l]), *mix_w,
                                      cache_k=cache_k[l], cache_v=cache_v[l], state=state_gla[l])
        y_s = y_s + mix
        y_s = y_s + hier_moe(rms_norm(y_s, norm2_gain[l]), *moe_w)
        kp_l.append(kp); vp_l.append(vp); sp_l.append(sp)
        ks_l.append(kn); vs_l.append(vn); ss_l.append(sn)
    return (y_p, y_s, jnp.stack(kp_l), jnp.stack(vp_l), jnp.stack(sp_l),
            jnp.stack(ks_l), jnp.stack(vs_l), jnp.stack(ss_l))
```

```python
import functools
import math

import jax
import jax.numpy as jnp
from jax import lax
from jax.experimental import pallas as pl
from jax.experimental.pallas import tpu as pltpu

F32 = jnp.float32
BF16 = jnp.bfloat16

D_MODEL = 1024
CHUNK = 64
A_HEADS = 4
A_DH = 64
A_DV = 128
A_WIDTH = A_HEADS * A_DV
G_HEADS = 4
G_DK = 64
G_DV = 128
G_WIDTH = G_HEADS * G_DV
G_RANK = 16
G_TAU = 16.0
N_GROUPS = 4
EXPERTS_PER_GROUP = 8
N_EXPERTS = N_GROUPS * EXPERTS_PER_GROUP
D_EXPERT = D_MODEL // 4
EPS = 1e-6

LANES = 128
NEG = -1e30
VMEM_LIMIT = 56 * 1024 * 1024

_C_AQ, _C_AK, _C_AV = 0, 512, 1024
_C_GQ, _C_GK, _C_GV = 1536, 1792, 2048
_C_GA, _C_GR = 2560, 2576
W_BIG = 3328

ATT_TQ = 256
ATT_TK = 256
ROUTE_OFF = N_GROUPS


def _cparams(sem):
    return pltpu.CompilerParams(dimension_semantics=sem, vmem_limit_bytes=VMEM_LIMIT)


def _nt(a, b):
    return lax.dot_general(a, b, (((1,), (1,)), ((), ())), preferred_element_type=F32)


def _tn(a, b):
    return lax.dot_general(a, b, (((0,), (0,)), ((), ())), preferred_element_type=F32)


def _dot(a, b):
    return jnp.dot(a, b, preferred_element_type=F32)


def _fold_kernel(wga_ref, wa2_ref, out_ref):
    out_ref[...] = jnp.dot(wga_ref[...], wa2_ref[...], preferred_element_type=F32,
                           precision=lax.Precision.HIGHEST)


def _fold(w_ga, w_a2):
    return pl.pallas_call(
        _fold_kernel,
        out_shape=jax.ShapeDtypeStruct((D_MODEL, G_HEADS * G_DK), F32),
        name="fold_gate",
    )(w_ga, w_a2)


def _headnorm(z, gain2):
    outs = []
    lane = lax.broadcasted_iota(jnp.int32, (z.shape[0], LANES), 1)
    lo = lane < A_DH
    for c in range(z.shape[1] // LANES):
        x = z[:, c * LANES:(c + 1) * LANES]
        xx = x * x
        s_lo = jnp.sum(jnp.where(lo, xx, 0.0), axis=-1, keepdims=True)
        s_hi = jnp.sum(jnp.where(lo, 0.0, xx), axis=-1, keepdims=True)
        r = jnp.where(lo, lax.rsqrt(s_lo * (1.0 / A_DH) + EPS), lax.rsqrt(s_hi * (1.0 / A_DH) + EPS))
        outs.append((x * r) * gain2)
    return outs


def _inproj_kernel(x_ref, g1_ref, w_ref, qg_ref, kg_ref, ba_ref,
                   q_ref, k_ref, v_ref, qk_ref, gv_ref, gr_ref, la_ref):
    x = x_ref[...]
    ms = jnp.mean(x * x, axis=-1, keepdims=True)
    h = ((x * lax.rsqrt(ms + EPS)) * g1_ref[...]).astype(BF16)

    def seg(lo, hi):
        return _dot(h, w_ref[:, lo:hi])

    for c, y in enumerate(_headnorm(seg(0, 512), qg_ref[...])):
        q_ref[:, c * LANES:(c + 1) * LANES] = (y * (A_DH ** -0.5)).astype(BF16)
    for c, y in enumerate(_headnorm(seg(512, 1024), kg_ref[...])):
        k_ref[:, c * LANES:(c + 1) * LANES] = y
    v_ref[...] = seg(1024, 1536)
    qk_ref[...] = seg(1536, 2048)
    gv_ref[...] = seg(2048, 2560)
    gr_ref[...] = seg(2560, 3072)
    zl = seg(3072, 3328) + ba_ref[...]
    la_ref[...] = (jnp.minimum(zl, 0.0) - jnp.log1p(jnp.exp(-jnp.abs(zl)))) * (1.0 / G_TAU)


def _inproj(x, g1, w_big, qg2, kg2, ba):
    n = x.shape[0]
    tm = min(512, n)
    row = lambda w: pl.BlockSpec((tm, w), lambda i: (i, 0))
    full = lambda a: pl.BlockSpec(a.shape, lambda i: (0, 0))
    outs = [(512, BF16), (512, F32), (512, F32), (512, F32), (512, F32), (512, F32), (256, F32)]
    return pl.pallas_call(
        _inproj_kernel,
        grid=(n // tm,),
        in_specs=[row(D_MODEL), full(g1), full(w_big), full(qg2), full(kg2), full(ba)],
        out_specs=[row(w) for w, _ in outs],
        out_shape=[jax.ShapeDtypeStruct((n, w), dt) for w, dt in outs],
        compiler_params=_cparams(("parallel",)),
        name="inproj",
    )(x, g1, w_big, qg2, kg2, ba)


def _diff_lambda(lq1_ref, lk1_ref, lq2_ref, lk2_ref, lam_init):
    a = jnp.sum(lq1_ref[...] * lk1_ref[...], axis=-1, keepdims=True)
    b = jnp.sum(lq2_ref[...] * lk2_ref[...], axis=-1, keepdims=True)
    return jnp.exp(a) - jnp.exp(b) + lam_init


def _attn_prompt_kernel(lq1_ref, lk1_ref, lq2_ref, lk2_ref, gain_ref, q_ref, k_ref, v_ref,
                        o_ref, kz1_s, kz2_s, vt_s, acc1_s, acc2_s, *, seq, lam_init):
    tq, tk = ATT_TQ, ATT_TK
    nq = seq // tq
    lam = _diff_lambda(lq1_ref, lk1_ref, lq2_ref, lk2_ref, lam_init)
    lane = lax.broadcasted_iota(jnp.int32, (tk, LANES), 1)

    def prep(j, carry):
        rows = pl.ds(pl.multiple_of(j * tk, tk), tk)
        kb = k_ref[rows, :]
        kz1_s[j] = jnp.where(lane < A_DH, kb, 0.0).astype(BF16)
        kz2_s[j] = jnp.where(lane < A_DH, 0.0, kb).astype(BF16)
        vt_s[j] = v_ref[rows, :].T.astype(BF16)
        return carry

    lax.fori_loop(0, seq // tk, prep, 0)

    kk = lax.broadcasted_iota(jnp.int32, (tk, tq), 0)
    qq = lax.broadcasted_iota(jnp.int32, (tk, tq), 1)
    visible = (kk // CHUNK) <= (qq // CHUNK)

    def kv_step(j, carry, qi, diagonal):
        out = []
        for kz_s, acc_s, (m, l) in ((kz1_s, acc1_s, carry[0]), (kz2_s, acc2_s, carry[1])):
            s = _nt(kz_s[j], qi)
            if diagonal:
                s = jnp.where(visible, s, NEG)
            m_new = jnp.maximum(m, jnp.max(s, axis=0, keepdims=True))
            alpha = jnp.exp(m - m_new)
            p = jnp.exp(s - m_new)
            l_new = alpha * l + jnp.sum(p, axis=0, keepdims=True)
            acc_s[...] = alpha * acc_s[...] + _dot(vt_s[j], p.astype(BF16))
            out.append((m_new, l_new))
        return tuple(out)

    def q_block(i, carry):
        rows = pl.ds(pl.multiple_of(i * tq, tq), tq)
        qi = q_ref[rows, :]
        acc1_s[...] = jnp.zeros_like(acc1_s)
        acc2_s[...] = jnp.zeros_like(acc2_s)
        init = ((jnp.full((1, tq), NEG, F32), jnp.zeros((1, tq), F32)),) * 2
        st = lax.fori_loop(0, i, lambda j, c: kv_step(j, c, qi, False), init)
        (_, l1), (_, l2) = kv_step(i, st, qi, True)
        ot = acc1_s[...] * (1.0 / l1) - lam * (acc2_s[...] * (1.0 / l2))
        ms = jnp.mean(ot * ot, axis=0, keepdims=True)
        ot = ((ot * lax.rsqrt(ms + EPS)) * gain_ref[...]) * (1.0 - lam_init)
        o_ref[rows, :] = ot.T.astype(o_ref.dtype)
        return carry

    lax.fori_loop(0, nq, q_block, 0)


def _attn_prompt(lams, gain_col, q, k, v, *, batch, seq, lam_init):
    nblk = seq // ATT_TK
    vec = pl.BlockSpec((1, A_DH), lambda b, h: (0, 0))
    head = pl.BlockSpec((seq, LANES), lambda b, h: (b, h))
    return pl.pallas_call(
        functools.partial(_attn_prompt_kernel, seq=seq, lam_init=lam_init),
        grid=(batch, A_HEADS),
        in_specs=[vec, vec, vec, vec, pl.BlockSpec((A_DV, 1), lambda b, h: (0, 0)), head, head, head],
        out_specs=head,
        out_shape=jax.ShapeDtypeStruct((batch * seq, A_WIDTH), BF16),
        scratch_shapes=[
            pltpu.VMEM((nblk, ATT_TK, LANES), BF16),
            pltpu.VMEM((nblk, ATT_TK, LANES), BF16),
            pltpu.VMEM((nblk, A_DV, ATT_TK), BF16),
            pltpu.VMEM((A_DV, ATT_TQ), F32),
            pltpu.VMEM((A_DV, ATT_TQ), F32),
        ],
        compiler_params=_cparams(("parallel", "parallel")),
        name="attn_prompt",
    )(*lams, gain_col, q, k, v)


def _attn_sample_kernel(lq1_ref, lk1_ref, lq2_ref, lk2_ref, gain_ref, q_ref, kn_ref, vn_ref,
                        kc_ref, vc_ref, o_ref, *, lam_init):
    lam = _diff_lambda(lq1_ref, lk1_ref, lq2_ref, lk2_ref, lam_init)
    q = q_ref[...]
    nq = q.shape[0]
    lane = lax.broadcasted_iota(jnp.int32, q.shape, 1)
    zero = jnp.zeros_like(q)
    qz = jnp.concatenate([jnp.where(lane < A_DH, q, zero), jnp.where(lane < A_DH, zero, q)], axis=0)
    sc = _nt(qz, kc_ref[...].astype(BF16))
    sn = _nt(qz, kn_ref[...].astype(BF16))
    m = jnp.maximum(jnp.max(sc, axis=-1, keepdims=True), jnp.max(sn, axis=-1, keepdims=True))
    ec = jnp.exp(sc - m)
    en = jnp.exp(sn - m)
    l = jnp.sum(ec, axis=-1, keepdims=True) + jnp.sum(en, axis=-1, keepdims=True)
    pv = _dot(ec.astype(BF16), vc_ref[...].astype(BF16)) + _dot(en.astype(BF16), vn_ref[...].astype(BF16))
    pv = pv * (1.0 / l)
    o = pv[:nq] - lam * pv[nq:]
    ms = jnp.mean(o * o, axis=-1, keepdims=True)
    o_ref[...] = (((o * lax.rsqrt(ms + EPS)) * gain_ref[...]) * (1.0 - lam_init)).astype(o_ref.dtype)


def _attn_sample(lams, gain_row, q, k, v, kc, vc, *, batch, seq, past, lam_init):
    vec = pl.BlockSpec((1, A_DH), lambda b, h: (0, 0))
    new = pl.BlockSpec((seq, LANES), lambda b, h: (b, h))
    old = pl.BlockSpec((past, LANES), lambda b, h: (b, h))
    return pl.pallas_call(
        functools.partial(_attn_sample_kernel, lam_init=lam_init),
        grid=(batch, A_HEADS),
        in_specs=[vec, vec, vec, vec, pl.BlockSpec((1, A_DV), lambda b, h: (0, 0)), new, new, new, old, old],
        out_specs=new,
        out_shape=jax.ShapeDtypeStruct((batch * seq, A_WIDTH), BF16),
        compiler_params=_cparams(("parallel", "parallel")),
        name="attn_sample",
    )(*lams, gain_row, q, k, v, kc, vc)


def _gla_kernel(qk_ref, la_ref, v_ref, gr_ref, gain_ref, s0_ref, go_ref, st_ref, st_s, *, chunk, rows_per_step):
    c = chunk

    @pl.when(pl.program_id(1) == 0)
    def _():
        st_s[...] = s0_ref[...]

    lane = lax.broadcasted_iota(jnp.int32, (c, LANES), 1)
    lo = lane < G_DK
    rr = lax.broadcasted_iota(jnp.int32, (c, c), 0)
    cc = lax.broadcasted_iota(jnp.int32, (c, c), 1)
    causal = rr >= cc
    tril = jnp.where(causal, 1.0, 0.0).astype(BF16)
    scale = G_DK ** -0.5

    def chunk_step(ci, carry):
        rows = pl.ds(pl.multiple_of(ci * c, c), c)
        for h in range(G_HEADS):
            cols = slice(h * LANES, (h + 1) * LANES)
            pair = slice((h // 2) * LANES, (h // 2 + 1) * LANES)
            qk = qk_ref[rows, cols]
            kq = pltpu.roll(qk, G_DK, 1)
            lag = la_ref[rows, pair]
            lar = pltpu.roll(lag, G_DK, 1)
            la2 = jnp.where(lo, lag, lar) if h % 2 == 0 else jnp.where(lo, lar, lag)
            la_hi = la2.astype(BF16)
            la_lo = (la2 - la_hi.astype(F32)).astype(BF16)
            b = _dot(tril, la_hi) + _dot(tril, la_lo)
            b_last = b[c - 1:c, :]
            b_mid = b[c // 2 - 1:c // 2, :]
            qt = (qk * jnp.exp(b - b_mid)) * scale
            kt = kq * jnp.exp(b_mid - b)
            qe = (qk * jnp.exp(b)) * scale
            kh = kq * jnp.exp(b_last - b)
            sc = jnp.where(causal, _nt(qt[:, :G_DK].astype(BF16), kt[:, :G_DK].astype(BF16)), 0.0)
            vb = v_ref[rows, cols].astype(BF16)
            st = st_s[h]
            o = _dot(sc.astype(BF16), vb) + _nt(qe[:, :G_DK].astype(BF16), st.astype(BF16))
            st_s[h] = jnp.exp(b_last)[:, :G_DK] * st + _tn(vb, kh[:, :G_DK].astype(BF16))
            ms = jnp.mean(o * o, axis=-1, keepdims=True)
            on = (o * lax.rsqrt(ms + EPS)) * gain_ref[...]
            g = gr_ref[rows, cols]
            go_ref[rows, cols] = (on * (g * jax.nn.sigmoid(g))).astype(go_ref.dtype)
        return carry

    lax.fori_loop(0, rows_per_step // c, chunk_step, 0)
    st_ref[...] = st_s[...]


def _gla(qk, la, v, gr, gain_row, s0t, *, batch, seq, chunk):
    lb = min(seq, 1024)
    nl = seq // lb
    wide = pl.BlockSpec((lb, G_WIDTH), lambda b, l: (b * nl + l, 0))
    state = pl.BlockSpec((None, G_HEADS, G_DV, G_DK), lambda b, l: (b, 0, 0, 0))
    return pl.pallas_call(
        functools.partial(_gla_kernel, chunk=chunk, rows_per_step=lb),
        grid=(batch, nl),
        in_specs=[wide, pl.BlockSpec((lb, G_HEADS * G_DK), lambda b, l: (b * nl + l, 0)), wide, wide,
                  pl.BlockSpec((1, G_DV), lambda b, l: (0, 0)), state],
        out_specs=[wide, state],
        out_shape=[jax.ShapeDtypeStruct((batch * seq, G_WIDTH), BF16),
                   jax.ShapeDtypeStruct((batch, G_HEADS, G_DV, G_DK), F32)],
        scratch_shapes=[pltpu.VMEM((G_HEADS, G_DV, G_DK), F32)],
        compiler_params=_cparams(("arbitrary", "arbitrary")),
        name="gla",
    )(qk, la, v, gr, gain_row, s0t)


def _route(z):
    lane = lax.broadcasted_iota(jnp.int32, z.shape, 1)
    big = jnp.int32(LANES)

    def first_argmax(vals, vmax):
        return jnp.min(jnp.where(vals == vmax, lane, big), axis=-1, keepdims=True)

    zg = jnp.where(lane < N_GROUPS, z, NEG)
    gmax = jnp.max(zg, axis=-1, keepdims=True)
    g_idx = first_argmax(zg, gmax)
    g_w = 1.0 / jnp.sum(jnp.exp(zg - gmax), axis=-1, keepdims=True)
    first = ROUTE_OFF + EXPERTS_PER_GROUP * g_idx
    ze = jnp.where(lane < first, NEG, jnp.where(lane < first + EXPERTS_PER_GROUP, z, NEG))
    v1 = jnp.max(ze, axis=-1, keepdims=True)
    i1 = first_argmax(ze, v1)
    ze2 = jnp.where(lane == i1, NEG, ze)
    v2 = jnp.max(ze2, axis=-1, keepdims=True)
    i2 = first_argmax(ze2, v2)
    t = jnp.exp(v2 - v1)
    w1 = g_w / (1.0 + t)
    w2 = w1 * t
    return jnp.where(lane == i1, w1, 0.0) + jnp.where(lane == i2, w2, 0.0)


def _outproj_kernel(ao_ref, go_ref, x_ref, woa_ref, wog_ref, g2_ref, wrh_ref, wrl_ref, br_ref,
                    y1_ref, h2_ref, comb_ref):
    y1 = x_ref[...] + _dot(ao_ref[...], woa_ref[...]) + _dot(go_ref[...], wog_ref[...])
    y1_ref[...] = y1
    ms = jnp.mean(y1 * y1, axis=-1, keepdims=True)
    h2 = (y1 * lax.rsqrt(ms + EPS)) * g2_ref[...]
    hh = h2.astype(BF16)
    h2_ref[...] = hh
    hl = (h2 - hh.astype(F32)).astype(BF16)
    z = _dot(hh, wrh_ref[...]) + _dot(hl, wrh_ref[...]) + _dot(hh, wrl_ref[...]) + br_ref[...]
    comb_ref[...] = _route(z)


def _outproj(ao, go, x, woa, wog, g2, wrh, wrl, br):
    n = x.shape[0]
    tm = min(512, n)
    row = lambda w: pl.BlockSpec((tm, w), lambda i: (i, 0))
    full = lambda a: pl.BlockSpec(a.shape, lambda i: (0, 0))
    return pl.pallas_call(
        _outproj_kernel,
        grid=(n // tm,),
        in_specs=[row(A_WIDTH), row(G_WIDTH), row(D_MODEL), full(woa), full(wog), full(g2),
                  full(wrh), full(wrl), full(br)],
        out_specs=[row(D_MODEL), row(D_MODEL), row(LANES)],
        out_shape=[jax.ShapeDtypeStruct((n, D_MODEL), F32), jax.ShapeDtypeStruct((n, D_MODEL), BF16),
                   jax.ShapeDtypeStruct((n, LANES), F32)],
        compiler_params=_cparams(("parallel",)),
        name="outproj_router",
    )(ao, go, x, woa, wog, g2, wrh, wrl, br)


def _moe_kernel(h2_ref, y1_ref, comb_ref, wgu_ref, wd_ref, out_ref):
    e = pl.program_id(1)

    @pl.when(e == 0)
    def _():
        out_ref[...] = y1_ref[...]

    gu = _dot(h2_ref[...], wgu_ref[...])
    g = gu[:, :D_EXPERT]
    u = gu[:, D_EXPERT:]
    a = ((g * jax.nn.sigmoid(g)) * u).astype(BF16)
    d = _dot(a, wd_ref[...])
    comb = comb_ref[...]
    lane = lax.broadcasted_iota(jnp.int32, comb.shape, 1)
    cw = jnp.sum(jnp.where(lane == e + ROUTE_OFF, comb, 0.0), axis=-1, keepdims=True)
    out_ref[...] += cw * d


def _moe(h2, y1, comb, wgu, wd):
    n = h2.shape[0]
    tm = min(1024, n)
    row = lambda w: pl.BlockSpec((tm, w), lambda i, e: (i, 0))
    return pl.pallas_call(
        _moe_kernel,
        grid=(n // tm, N_EXPERTS),
        in_specs=[row(D_MODEL), row(D_MODEL), row(LANES),
                  pl.BlockSpec((None, D_MODEL, 2 * D_EXPERT), lambda i, e: (e, 0, 0)),
                  pl.BlockSpec((None, D_EXPERT, D_MODEL), lambda i, e: (e, 0, 0))],
        out_specs=row(D_MODEL),
        out_shape=jax.ShapeDtypeStruct((n, D_MODEL), F32),
        compiler_params=_cparams(("parallel", "arbitrary")),
        name="moe",
    )(h2, y1, comb, wgu, wd)


def _layer_weights(l, norm1_gain, w_in, a_q_gain, a_k_gain, lambda_q1, lambda_k1, lambda_q2, lambda_k2,
                   a_out_gain, w_a2, b_a, g_out_gain, w_out, norm2_gain, w_group, b_group, w_erouter,
                   b_erouter, w_gate, w_up, w_down):
    w = w_in[l]
    w_la = _fold(w[:, _C_GA:_C_GR], w_a2[l])
    qk_cols = []
    for h in range(G_HEADS):
        qk_cols += [w[:, _C_GQ + h * G_DK:_C_GQ + (h + 1) * G_DK], w[:, _C_GK + h * G_DK:_C_GK + (h + 1) * G_DK]]
    w_big = jnp.concatenate([w[:, :_C_GQ]] + qk_cols + [w[:, _C_GV:_C_GA], w[:, _C_GR:], w_la], axis=1).astype(BF16)
    w_r = jnp.concatenate([w_group[l], w_erouter[l],
                           jnp.zeros((D_MODEL, LANES - N_GROUPS - N_EXPERTS), F32)], axis=1)
    w_rh = w_r.astype(BF16)
    w_rl = (w_r - w_rh.astype(F32)).astype(BF16)
    b_r = jnp.concatenate([b_group[l], b_erouter[l], jnp.zeros((LANES - N_GROUPS - N_EXPERTS,), F32)])[None, :]
    return dict(
        g1=norm1_gain[l][None, :], w_big=w_big,
        qg2=jnp.tile(a_q_gain[l], 2)[None, :], kg2=jnp.tile(a_k_gain[l], 2)[None, :],
        ba=b_a[l][None, :],
        lams=(lambda_q1[l][None, :], lambda_k1[l][None, :], lambda_q2[l][None, :], lambda_k2[l][None, :]),
        a_gain=a_out_gain[l], g_gain=g_out_gain[l][None, :],
        woa=w_out[l][:A_WIDTH].astype(BF16), wog=w_out[l][A_WIDTH:].astype(BF16),
        g2=norm2_gain[l][None, :], w_rh=w_rh, w_rl=w_rl, b_r=b_r,
        wgu=jnp.concatenate([w_gate[l], w_up[l]], axis=-1).astype(BF16), wd=w_down[l].astype(BF16),
    )


def _mix_and_moe(x, p, *, batch, seq, chunk, lam_init, cache=None, state=None):
    q, k, v, qk, gv, gr, la = _inproj(x, p["g1"], p["w_big"], p["qg2"], p["kg2"], p["ba"])
    if cache is None:
        ao = _attn_prompt(p["lams"], p["a_gain"][:, None], q, k, v, batch=batch, seq=seq, lam_init=lam_init)
        s0t = jnp.zeros((batch, G_HEADS, G_DV, G_DK), F32)
    else:
        kc, vc = cache
        ao = _attn_sample(p["lams"], p["a_gain"][None, :], q, k, v, kc, vc, batch=batch, seq=seq,
                          past=kc.shape[0] // batch, lam_init=lam_init)
        s0t = jnp.swapaxes(state, -1, -2)
    go, st = _gla(qk, la, gv, gr, p["g_gain"], s0t, batch=batch, seq=seq, chunk=chunk)
    y1, h2, comb = _outproj(ao, go, x, p["woa"], p["wog"], p["g2"], p["w_rh"], p["w_rl"], p["b_r"])
    y = _moe(h2, y1, comb, p["wgu"], p["wd"])
    return y, k, v, jnp.swapaxes(st, -1, -2)


def kernel(x_prompt, x_sample, cache_k, cache_v, state_gla, norm1_gain, w_in, a_q_gain, a_k_gain, lambda_q1, lambda_k1, lambda_q2, lambda_k2, a_out_gain, w_a2, b_a, g_out_gain, w_out, norm2_gain, w_group, b_group, w_erouter, b_erouter, w_gate, w_up, w_down):
    weights = (norm1_gain, w_in, a_q_gain, a_k_gain, lambda_q1, lambda_k1, lambda_q2, lambda_k2, a_out_gain,
               w_a2, b_a, g_out_gain, w_out, norm2_gain, w_group, b_group, w_erouter, b_erouter, w_gate,
               w_up, w_down)
    depth = w_in.shape[0]
    pb, pl_, d = x_prompt.shape
    sb, sl, _ = x_sample.shape
    past = cache_k.shape[2]
    y_p = x_prompt.reshape(pb * pl_, d)
    y_s = x_sample.reshape(sb * sl, d)
    outs = [[] for _ in range(6)]
    for l in range(depth):
        lam_init = 0.8 - 0.6 * math.exp(-0.3 * l)
        p = _layer_weights(l, *weights)
        y_p, kp, vp, sp = _mix_and_moe(y_p, p, batch=pb, seq=pl_, chunk=CHUNK, lam_init=lam_init)
        cache = (cache_k[l].reshape(sb * past, A_WIDTH), cache_v[l].reshape(sb * past, A_WIDTH))
        y_s, kn, vn, sn = _mix_and_moe(y_s, p, batch=sb, seq=sl, chunk=sl, lam_init=lam_init,
                                       cache=cache, state=state_gla[l])
        outs[0].append(kp.reshape(pb, pl_, A_HEADS, 2, A_DH))
        outs[1].append(vp.reshape(pb, pl_, A_HEADS, A_DV))
        outs[2].append(sp)
        outs[3].append(kn.reshape(sb, sl, A_HEADS, 2, A_DH))
        outs[4].append(vn.reshape(sb, sl, A_HEADS, A_DV))
        outs[5].append(sn)
    return (y_p.reshape(pb, pl_, d), y_s.reshape(sb, sl, d)) + tuple(jnp.stack(o) for o in outs)
```

```python
import functools
import math

import jax
import jax.numpy as jnp
from jax import lax
from jax.experimental import pallas as pl
from jax.experimental.pallas import tpu as pltpu

F32 = jnp.float32
BF16 = jnp.bfloat16

D_MODEL = 1024
CHUNK = 64
A_HEADS = 4
A_DH = 64
A_DV = 128
A_WIDTH = A_HEADS * A_DV
G_HEADS = 4
G_DK = 64
G_DV = 128
G_WIDTH = G_HEADS * G_DV
G_RANK = 16
G_TAU = 16.0
N_GROUPS = 4
EXPERTS_PER_GROUP = 8
N_EXPERTS = N_GROUPS * EXPERTS_PER_GROUP
D_EXPERT = D_MODEL // 4
EPS = 1e-6

LANES = 128
NEG = -1e30
VMEM_LIMIT = 56 * 1024 * 1024

_C_AQ, _C_AK, _C_AV = 0, 512, 1024
_C_GQ, _C_GK, _C_GV = 1536, 1792, 2048
_C_GA, _C_GR = 2560, 2576
W_BIG = 3328

INPROJ_TM = 512
ATT_TQ = 512
ATT_TK = INPROJ_TM
ATT_RG = 256
LOG2E = 1.4426950408889634
ATT_SAFE_BOUND = 43.0 * LOG2E
ROUTE_OFF = N_GROUPS


def _cparams(sem):
    return pltpu.CompilerParams(dimension_semantics=sem, vmem_limit_bytes=VMEM_LIMIT)


def _nt(a, b):
    return lax.dot_general(a, b, (((1,), (1,)), ((), ())), preferred_element_type=F32)


def _tn(a, b):
    return lax.dot_general(a, b, (((0,), (0,)), ((), ())), preferred_element_type=F32)


def _dot(a, b):
    return jnp.dot(a, b, preferred_element_type=F32)


def _fold_kernel(wga_ref, wa2_ref, out_ref):
    out_ref[...] = jnp.dot(wga_ref[...], wa2_ref[...], preferred_element_type=F32,
                           precision=lax.Precision.HIGHEST)


def _fold(w_ga, w_a2):
    return pl.pallas_call(
        _fold_kernel,
        out_shape=jax.ShapeDtypeStruct((D_MODEL, G_HEADS * G_DK), F32),
        name="fold_gate",
    )(w_ga, w_a2)


def _headnorm(z, gain2):
    outs = []
    lane = lax.broadcasted_iota(jnp.int32, (z.shape[0], LANES), 1)
    lo = lane < A_DH
    for c in range(z.shape[1] // LANES):
        x = z[:, c * LANES:(c + 1) * LANES]
        xx = x * x
        s_lo = jnp.sum(jnp.where(lo, xx, 0.0), axis=-1, keepdims=True)
        s_hi = jnp.sum(jnp.where(lo, 0.0, xx), axis=-1, keepdims=True)
        r = jnp.where(lo, lax.rsqrt(s_lo * (1.0 / A_DH) + EPS), lax.rsqrt(s_hi * (1.0 / A_DH) + EPS))
        outs.append((x * r) * gain2)
    return outs


def _inproj_kernel(x_ref, g1_ref, w_ref, qg_ref, kg_ref, ba_ref, *out_refs, final_layout):
    if final_layout:
        q_ref, kt_ref, ktb_ref, v4_ref, vb_ref, qk_ref, gv_ref, gr_ref, la_ref = out_refs
    else:
        q_ref, k_ref, v_ref, qk_ref, gv_ref, gr_ref, la_ref = out_refs
    x = x_ref[...]
    tm = x.shape[0]
    ms = jnp.mean(x * x, axis=-1, keepdims=True)
    h = ((x * lax.rsqrt(ms + EPS)) * g1_ref[...]).astype(BF16)

    def seg(lo, hi):
        return _dot(h, w_ref[:, lo:hi])

    for c, y in enumerate(_headnorm(seg(0, 512), qg_ref[...])):
        q_ref[:, c * LANES:(c + 1) * LANES] = (y * (A_DH ** -0.5 * LOG2E)).astype(BF16)
    for c, y in enumerate(_headnorm(seg(512, 1024), kg_ref[...])):
        cols = slice(c * LANES, (c + 1) * LANES)
        if final_layout:
            yt = y.T
            kt_ref[cols, :] = yt
            ktb_ref[cols, :] = yt.astype(BF16)
        else:
            k_ref[:, cols] = y
    v = seg(1024, 1536)
    if final_layout:
        vb_ref[...] = v.astype(BF16)
        for c in range(A_HEADS):
            v4_ref[pl.ds(c, tm, stride=A_HEADS), :] = v[:, c * LANES:(c + 1) * LANES]
    else:
        v_ref[...] = v
    qk_ref[...] = seg(1536, 2048)
    gv_ref[...] = seg(2048, 2560)
    gr_ref[...] = seg(2560, 3072)
    zl = seg(3072, 3328) + ba_ref[...]
    la_ref[...] = (jnp.minimum(zl, 0.0) - jnp.log1p(jnp.exp(-jnp.abs(zl)))) * (1.0 / G_TAU)


def _inproj(x, g1, w_big, qg2, kg2, ba, *, batch, seq, final_layout):
    n = x.shape[0]
    tm = min(INPROJ_TM, seq)
    nlt = seq // tm
    row = lambda w: pl.BlockSpec((tm, w), lambda i: (i, 0))
    full = lambda a: pl.BlockSpec(a.shape, lambda i: (0, 0))
    tail = [(row(512), (n, 512), F32)] * 3 + [(row(256), (n, 256), F32)]
    if final_layout:
        outs = [
            (row(512), (n, 512), BF16),
            (pl.BlockSpec((None, 512, tm), lambda i: (i // nlt, 0, i % nlt)), (batch, 512, seq), F32),
            (pl.BlockSpec((None, None, 512, tm), lambda i: (i // nlt, i % nlt, 0, 0)),
             (batch, nlt, 512, tm), BF16),
            (pl.BlockSpec((tm * A_HEADS, LANES), lambda i: (i, 0)), (n * A_HEADS, LANES), F32),
            (row(512), (n, 512), BF16),
        ] + tail
    else:
        outs = [(row(512), (n, 512), BF16), (row(512), (n, 512), F32), (row(512), (n, 512), F32)] + tail
    return pl.pallas_call(
        functools.partial(_inproj_kernel, final_layout=final_layout),
        grid=(n // tm,),
        in_specs=[row(D_MODEL), full(g1), full(w_big), full(qg2), full(kg2), full(ba)],
        out_specs=[o[0] for o in outs],
        out_shape=[jax.ShapeDtypeStruct(o[1], o[2]) for o in outs],
        compiler_params=_cparams(("parallel",)),
        name="inproj",
    )(x, g1, w_big, qg2, kg2, ba)


def _diff_lambda(lq1_ref, lk1_ref, lq2_ref, lk2_ref, lam_init):
    a = jnp.sum(lq1_ref[...] * lk1_ref[...], axis=-1, keepdims=True)
    b = jnp.sum(lq2_ref[...] * lk2_ref[...], axis=-1, keepdims=True)
    return jnp.exp(a) - jnp.exp(b) + lam_init


def _attn_prompt_kernel(lq1_ref, lk1_ref, lq2_ref, lk2_ref, gain_ref, q_ref, kt_ref, v_ref,
                        o_ref, acc_s, *, seq, lam_init):
    tq, tk, rg = ATT_TQ, ATT_TK, ATT_RG
    nrg = 2 * tq // rg
    lam = _diff_lambda(lq1_ref, lk1_ref, lq2_ref, lk2_ref, lam_init)
    lane = lax.broadcasted_iota(jnp.int32, (tq, LANES), 1)
    lo = lane < A_DH
    ones = jnp.ones((tk, LANES), BF16)
    qq = lax.broadcasted_iota(jnp.int32, (rg, tk), 0)
    kk = lax.broadcasted_iota(jnp.int32, (rg, tk), 1)
    visible = [(kk // CHUNK) <= ((qq + part * rg) // CHUNK) for part in range(tq // rg)]
    unit = jnp.where(lax.broadcasted_iota(jnp.int32, (LANES, tk), 0) == 0, 1.0, 0.0).astype(BF16)

    def key_norms(j, c):
        kb = kt_ref[j].astype(F32)
        sq = kb * kb
        n1 = jnp.max(jnp.sum(sq[:A_DH], axis=0, keepdims=True), axis=1, keepdims=True)
        n2 = jnp.max(jnp.sum(sq[A_DH:], axis=0, keepdims=True), axis=1, keepdims=True)
        return jnp.maximum(c[0], n1), jnp.maximum(c[1], n2)

    zero11 = jnp.zeros((1, 1), F32)
    k1sq, k2sq = lax.fori_loop(0, seq // tk, key_norms, (zero11, zero11))

    def values(j):
        vb = v_ref[pl.ds(pl.multiple_of(j * tk, tk), tk), :]
        return jnp.concatenate([vb, ones], axis=1)

    def shifted_step(j, carry, qx, diagonal):
        ktx = jnp.concatenate([kt_ref[j], unit], axis=0)
        vx = values(j)
        for g in range(nrg):
            rows = slice(g * rg, (g + 1) * rg)
            p = jnp.exp2(_dot(qx[rows], ktx))
            if diagonal:
                p = jnp.where(visible[g % (tq // rg)], p, 0.0)
            acc_s[rows, :] += _dot(p.astype(BF16), vx)
        return carry

    def running_max_step(j, ms, qz, diagonal):
        kt = kt_ref[j]
        vx = values(j)
        out = []
        for g in range(nrg):
            rows = slice(g * rg, (g + 1) * rg)
            s = _dot(qz[rows], kt)
            if diagonal:
                s = jnp.where(visible[g % (tq // rg)], s, NEG)
            m_new = jnp.maximum(ms[g], jnp.max(s, axis=-1, keepdims=True))
            alpha = jnp.exp2(ms[g] - m_new)
            p = jnp.exp2(s - m_new).astype(BF16)
            acc_s[rows, :] = alpha * acc_s[rows, :] + _dot(p, vx)
            out.append(m_new)
        return tuple(out)

    def q_block(i, carry):
        qrows = pl.ds(pl.multiple_of(i * tq, tq), tq)
        qi = q_ref[qrows, :]
        zero = jnp.zeros_like(qi)
        q1 = jnp.where(lo, qi, zero)
        q2 = jnp.where(lo, zero, qi)
        qf = qi.astype(F32)
        sq = qf * qf
        shift1 = jnp.sqrt(jnp.sum(jnp.where(lo, sq, 0.0), axis=-1, keepdims=True) * k1sq)
        shift2 = jnp.sqrt(jnp.sum(jnp.where(lo, 0.0, sq), axis=-1, keepdims=True) * k2sq)
        safe = jnp.max(jnp.maximum(shift1, shift2)) <= ATT_SAFE_BOUND
        acc_s[...] = jnp.zeros_like(acc_s)

        @pl.when(safe)
        def _():
            x1 = jnp.where(lane == 0, -shift1, 0.0).astype(BF16)
            x2 = jnp.where(lane == 0, -shift2, 0.0).astype(BF16)
            qx = jnp.concatenate([jnp.concatenate([q1, x1], axis=1), jnp.concatenate([q2, x2], axis=1)], axis=0)
            lax.fori_loop(0, i, lambda j, c: shifted_step(j, c, qx, False), 0)
            shifted_step(i, 0, qx, True)

        @pl.when(jnp.logical_not(safe))
        def _():
            qz = jnp.concatenate([q1, q2], axis=0)
            ms = (jnp.full((rg, 1), NEG, F32),) * nrg
            ms = lax.fori_loop(0, i, lambda j, c: running_max_step(j, c, qz, False), ms)
            running_max_step(i, ms, qz, True)

        a1 = acc_s[:tq, :]
        a2 = acc_s[tq:, :]
        o = a1[:, :A_DV] / a1[:, A_DV:] - lam * (a2[:, :A_DV] / a2[:, A_DV:])
        msq = jnp.mean(o * o, axis=-1, keepdims=True)
        o_ref[qrows, :] = (((o * lax.rsqrt(msq + EPS)) * gain_ref[...]) * (1.0 - lam_init)).astype(o_ref.dtype)
        return carry

    lax.fori_loop(0, seq // tq, q_block, 0)


def _attn_prompt(lams, gain_row, q, ktb, v, *, batch, seq, lam_init):
    nkv = seq // ATT_TK
    vec = pl.BlockSpec((1, A_DH), lambda b, h: (0, 0))
    head = pl.BlockSpec((seq, LANES), lambda b, h: (b, h))
    return pl.pallas_call(
        functools.partial(_attn_prompt_kernel, seq=seq, lam_init=lam_init),
        grid=(batch, A_HEADS),
        in_specs=[vec, vec, vec, vec, pl.BlockSpec((1, A_DV), lambda b, h: (0, 0)), head,
                  pl.BlockSpec((None, nkv, LANES, ATT_TK), lambda b, h: (b, 0, h, 0)), head],
        out_specs=head,
        out_shape=jax.ShapeDtypeStruct((batch * seq, A_WIDTH), BF16),
        scratch_shapes=[pltpu.VMEM((2 * ATT_TQ, A_DV + LANES), F32)],
        compiler_params=_cparams(("parallel", "parallel")),
        name="attn_prompt",
    )(*lams, gain_row, q, ktb, v)


def _attn_sample_kernel(lq1_ref, lk1_ref, lq2_ref, lk2_ref, gain_ref, q_ref, kn_ref, vn_ref,
                        kct_ref, vc_ref, o_ref, *, past, lam_init):
    lam = _diff_lambda(lq1_ref, lk1_ref, lq2_ref, lk2_ref, lam_init)
    q = q_ref[...]
    nq = q.shape[0]
    lane = lax.broadcasted_iota(jnp.int32, q.shape, 1)
    zero = jnp.zeros_like(q)
    qz = jnp.concatenate([jnp.where(lane < A_DH, q, zero), jnp.where(lane < A_DH, zero, q)], axis=0)
    vc = vc_ref[pl.ds(pl.program_id(1), past, stride=A_HEADS), :].astype(BF16)
    sc = _dot(qz, kct_ref[...].astype(BF16))
    sn = _nt(qz, kn_ref[...].astype(BF16))
    m = jnp.maximum(jnp.max(sc, axis=-1, keepdims=True), jnp.max(sn, axis=-1, keepdims=True))
    ec = jnp.exp2(sc - m)
    en = jnp.exp2(sn - m)
    l = jnp.sum(ec, axis=-1, keepdims=True) + jnp.sum(en, axis=-1, keepdims=True)
    pv = _dot(ec.astype(BF16), vc) + _dot(en.astype(BF16), vn_ref[...].astype(BF16))
    pv = pv * (1.0 / l)
    o = pv[:nq] - lam * pv[nq:]
    ms = jnp.mean(o * o, axis=-1, keepdims=True)
    o_ref[...] = (((o * lax.rsqrt(ms + EPS)) * gain_ref[...]) * (1.0 - lam_init)).astype(o_ref.dtype)


def _attn_sample(lams, gain_row, q, k, v, kct, vc, *, batch, seq, past, lam_init):
    vec = pl.BlockSpec((1, A_DH), lambda b, h: (0, 0))
    new = pl.BlockSpec((seq, LANES), lambda b, h: (b, h))
    return pl.pallas_call(
        functools.partial(_attn_sample_kernel, past=past, lam_init=lam_init),
        grid=(batch, A_HEADS),
        in_specs=[vec, vec, vec, vec, pl.BlockSpec((1, A_DV), lambda b, h: (0, 0)), new, new, new,
                  pl.BlockSpec((None, LANES, past), lambda b, h: (b, h, 0)),
                  pl.BlockSpec((past * A_HEADS, LANES), lambda b, h: (b, 0))],
        out_specs=new,
        out_shape=jax.ShapeDtypeStruct((batch * seq, A_WIDTH), BF16),
        compiler_params=_cparams(("parallel", "arbitrary")),
        name="attn_sample",
    )(*lams, gain_row, q, k, v, kct, vc)


def _gla_kernel(qk_ref, la_ref, v_ref, gr_ref, gain_ref, s0_ref, go_ref, st_ref, st_s, *, chunk, rows_per_step):
    c = chunk

    @pl.when(pl.program_id(1) == 0)
    def _():
        st_s[...] = s0_ref[...]

    lane = lax.broadcasted_iota(jnp.int32, (c, LANES), 1)
    lo = lane < G_DK
    rr = lax.broadcasted_iota(jnp.int32, (c, c), 0)
    cc = lax.broadcasted_iota(jnp.int32, (c, c), 1)
    causal = rr >= cc
    tril = jnp.where(causal, 1.0, 0.0).astype(BF16)
    scale = G_DK ** -0.5

    def chunk_step(ci, carry):
        rows = pl.ds(pl.multiple_of(ci * c, c), c)
        for h in range(G_HEADS):
            cols = slice(h * LANES, (h + 1) * LANES)
            pair = slice((h // 2) * LANES, (h // 2 + 1) * LANES)
            qk = qk_ref[rows, cols]
            kq = pltpu.roll(qk, G_DK, 1)
            lag = la_ref[rows, pair]
            lar = pltpu.roll(lag, G_DK, 1)
            la2 = jnp.where(lo, lag, lar) if h % 2 == 0 else jnp.where(lo, lar, lag)
            la_hi = la2.astype(BF16)
            la_lo = (la2 - la_hi.astype(F32)).astype(BF16)
            b = _dot(tril, la_hi) + _dot(tril, la_lo)
            b_last = b[c - 1:c, :]
            b_mid = b[c // 2 - 1:c // 2, :]
            qt = (qk * jnp.exp(b - b_mid)) * scale
            kt = kq * jnp.exp(b_mid - b)
            qe = (qk * jnp.exp(b)) * scale
            kh = kq * jnp.exp(b_last - b)
            sc = jnp.where(causal, _nt(qt[:, :G_DK].astype(BF16), kt[:, :G_DK].astype(BF16)), 0.0)
            vb = v_ref[rows, cols].astype(BF16)
            st = st_s[h]
            o = _dot(sc.astype(BF16), vb) + _nt(qe[:, :G_DK].astype(BF16), st.astype(BF16))
            st_s[h] = jnp.exp(b_last)[:, :G_DK] * st + _tn(vb, kh[:, :G_DK].astype(BF16))
            ms = jnp.mean(o * o, axis=-1, keepdims=True)
            on = (o * lax.rsqrt(ms + EPS)) * gain_ref[...]
            g = gr_ref[rows, cols]
            go_ref[rows, cols] = (on * (g * jax.nn.sigmoid(g))).astype(go_ref.dtype)
        return carry

    lax.fori_loop(0, rows_per_step // c, chunk_step, 0)
    st_ref[...] = st_s[...]


def _gla(qk, la, v, gr, gain_row, s0t, *, batch, seq, chunk):
    lb = min(seq, 1024)
    nl = seq // lb
    wide = pl.BlockSpec((lb, G_WIDTH), lambda b, l: (b * nl + l, 0))
    state = pl.BlockSpec((None, G_HEADS, G_DV, G_DK), lambda b, l: (b, 0, 0, 0))
    return pl.pallas_call(
        functools.partial(_gla_kernel, chunk=chunk, rows_per_step=lb),
        grid=(batch, nl),
        in_specs=[wide, pl.BlockSpec((lb, G_HEADS * G_DK), lambda b, l: (b * nl + l, 0)), wide, wide,
                  pl.BlockSpec((1, G_DV), lambda b, l: (0, 0)), state],
        out_specs=[wide, state],
        out_shape=[jax.ShapeDtypeStruct((batch * seq, G_WIDTH), BF16),
                   jax.ShapeDtypeStruct((batch, G_HEADS, G_DV, G_DK), F32)],
        scratch_shapes=[pltpu.VMEM((G_HEADS, G_DV, G_DK), F32)],
        compiler_params=_cparams(("arbitrary", "arbitrary")),
        name="gla",
    )(qk, la, v, gr, gain_row, s0t)


def _route(z):
    lane = lax.broadcasted_iota(jnp.int32, z.shape, 1)
    big = jnp.int32(LANES)

    def first_argmax(vals, vmax):
        return jnp.min(jnp.where(vals == vmax, lane, big), axis=-1, keepdims=True)

    zg = jnp.where(lane < N_GROUPS, z, NEG)
    gmax = jnp.max(zg, axis=-1, keepdims=True)
    g_idx = first_argmax(zg, gmax)
    g_w = 1.0 / jnp.sum(jnp.exp(zg - gmax), axis=-1, keepdims=True)
    first = ROUTE_OFF + EXPERTS_PER_GROUP * g_idx
    ze = jnp.where(lane < first, NEG, jnp.where(lane < first + EXPERTS_PER_GROUP, z, NEG))
    v1 = jnp.max(ze, axis=-1, keepdims=True)
    i1 = first_argmax(ze, v1)
    ze2 = jnp.where(lane == i1, NEG, ze)
    v2 = jnp.max(ze2, axis=-1, keepdims=True)
    i2 = first_argmax(ze2, v2)
    t = jnp.exp(v2 - v1)
    w1 = g_w / (1.0 + t)
    w2 = w1 * t
    return jnp.where(lane == i1, w1, 0.0) + jnp.where(lane == i2, w2, 0.0)


def _outproj_kernel(ao_ref, go_ref, x_ref, woa_ref, wog_ref, g2_ref, wrh_ref, wrl_ref, br_ref,
                    y1_ref, h2_ref, comb_ref):
    y1 = x_ref[...] + _dot(ao_ref[...], woa_ref[...]) + _dot(go_ref[...], wog_ref[...])
    y1_ref[...] = y1
    ms = jnp.mean(y1 * y1, axis=-1, keepdims=True)
    h2 = (y1 * lax.rsqrt(ms + EPS)) * g2_ref[...]
    hh = h2.astype(BF16)
    h2_ref[...] = hh
    hl = (h2 - hh.astype(F32)).astype(BF16)
    z = _dot(hh, wrh_ref[...]) + _dot(hl, wrh_ref[...]) + _dot(hh, wrl_ref[...]) + br_ref[...]
    comb_ref[...] = _route(z)


def _outproj(ao, go, x, woa, wog, g2, wrh, wrl, br):
    n = x.shape[0]
    tm = min(512, n)
    row = lambda w: pl.BlockSpec((tm, w), lambda i: (i, 0))
    full = lambda a: pl.BlockSpec(a.shape, lambda i: (0, 0))
    return pl.pallas_call(
        _outproj_kernel,
        grid=(n // tm,),
        in_specs=[row(A_WIDTH), row(G_WIDTH), row(D_MODEL), full(woa), full(wog), full(g2),
                  full(wrh), full(wrl), full(br)],
        out_specs=[row(D_MODEL), row(D_MODEL), row(LANES)],
        out_shape=[jax.ShapeDtypeStruct((n, D_MODEL), F32), jax.ShapeDtypeStruct((n, D_MODEL), BF16),
                   jax.ShapeDtypeStruct((n, LANES), F32)],
        compiler_params=_cparams(("parallel",)),
        name="outproj_router",
    )(ao, go, x, woa, wog, g2, wrh, wrl, br)


def _moe_kernel(h2_ref, y1_ref, comb_ref, wgu_ref, wd_ref, out_ref):
    e = pl.program_id(1)

    @pl.when(e == 0)
    def _():
        out_ref[...] = y1_ref[...]

    gu = _dot(h2_ref[...], wgu_ref[...])
    g = gu[:, :D_EXPERT]
    u = gu[:, D_EXPERT:]
    a = ((g * jax.nn.sigmoid(g)) * u).astype(BF16)
    d = _dot(a, wd_ref[...])
    comb = comb_ref[...]
    lane = lax.broadcasted_iota(jnp.int32, comb.shape, 1)
    cw = jnp.sum(jnp.where(lane == e + ROUTE_OFF, comb, 0.0), axis=-1, keepdims=True)
    out_ref[...] += cw * d


def _moe(h2, y1, comb, wgu, wd):
    n = h2.shape[0]
    tm = min(1024, n)
    row = lambda w: pl.BlockSpec((tm, w), lambda i, e: (i, 0))
    return pl.pallas_call(
        _moe_kernel,
        grid=(n // tm, N_EXPERTS),
        in_specs=[row(D_MODEL), row(D_MODEL), row(LANES),
                  pl.BlockSpec((None, D_MODEL, 2 * D_EXPERT), lambda i, e: (e, 0, 0)),
                  pl.BlockSpec((None, D_EXPERT, D_MODEL), lambda i, e: (e, 0, 0))],
        out_specs=row(D_MODEL),
        out_shape=jax.ShapeDtypeStruct((n, D_MODEL), F32),
        compiler_params=_cparams(("parallel", "arbitrary")),
        name="moe",
    )(h2, y1, comb, wgu, wd)


def _layer_weights(l, norm1_gain, w_in, a_q_gain, a_k_gain, lambda_q1, lambda_k1, lambda_q2, lambda_k2,
                   a_out_gain, w_a2, b_a, g_out_gain, w_out, norm2_gain, w_group, b_group, w_erouter,
                   b_erouter, w_gate, w_up, w_down):
    w = w_in[l]
    w_la = _fold(w[:, _C_GA:_C_GR], w_a2[l])
    qk_cols = []
    for h in range(G_HEADS):
        qk_cols += [w[:, _C_GQ + h * G_DK:_C_GQ + (h + 1) * G_DK], w[:, _C_GK + h * G_DK:_C_GK + (h + 1) * G_DK]]
    w_big = jnp.concatenate([w[:, :_C_GQ]] + qk_cols + [w[:, _C_GV:_C_GA], w[:, _C_GR:], w_la], axis=1).astype(BF16)
    w_r = jnp.concatenate([w_group[l], w_erouter[l],
                           jnp.zeros((D_MODEL, LANES - N_GROUPS - N_EXPERTS), F32)], axis=1)
    w_rh = w_r.astype(BF16)
    w_rl = (w_r - w_rh.astype(F32)).astype(BF16)
    b_r = jnp.concatenate([b_group[l], b_erouter[l], jnp.zeros((LANES - N_GROUPS - N_EXPERTS,), F32)])[None, :]
    return dict(
        g1=norm1_gain[l][None, :], w_big=w_big,
        qg2=jnp.tile(a_q_gain[l], 2)[None, :], kg2=jnp.tile(a_k_gain[l], 2)[None, :],
        ba=b_a[l][None, :],
        lams=(lambda_q1[l][None, :], lambda_k1[l][None, :], lambda_q2[l][None, :], lambda_k2[l][None, :]),
        a_gain=a_out_gain[l][None, :], g_gain=g_out_gain[l][None, :],
        woa=w_out[l][:A_WIDTH].astype(BF16), wog=w_out[l][A_WIDTH:].astype(BF16),
        g2=norm2_gain[l][None, :], w_rh=w_rh, w_rl=w_rl, b_r=b_r,
        wgu=jnp.concatenate([w_gate[l], w_up[l]], axis=-1).astype(BF16), wd=w_down[l].astype(BF16),
    )


def _mix_and_moe(x, p, *, batch, seq, chunk, lam_init, cache=None, state=None):
    prompt = cache is None
    res = _inproj(x, p["g1"], p["w_big"], p["qg2"], p["kg2"], p["ba"], batch=batch, seq=seq, final_layout=prompt)
    if prompt:
        q, kt, ktb, v4, vb, qk, gv, gr, la = res
        ao = _attn_prompt(p["lams"], p["a_gain"], q, ktb, vb, batch=batch, seq=seq, lam_init=lam_init)
        s0t = jnp.zeros((batch, G_HEADS, G_DV, G_DK), F32)
        k_out = jnp.transpose(kt.reshape(batch, A_HEADS, 2, A_DH, seq), (0, 4, 1, 2, 3))
        v_out = v4.reshape(batch, seq, A_HEADS, A_DV)
    else:
        q, k, v, qk, gv, gr, la = res
        kct, vc = cache
        ao = _attn_sample(p["lams"], p["a_gain"], q, k, v, kct, vc, batch=batch, seq=seq,
                          past=kct.shape[-1], lam_init=lam_init)
        s0t = jnp.swapaxes(state, -1, -2)
        k_out = k.reshape(batch, seq, A_HEADS, 2, A_DH)
        v_out = v.reshape(batch, seq, A_HEADS, A_DV)
    go, st = _gla(qk, la, gv, gr, p["g_gain"], s0t, batch=batch, seq=seq, chunk=chunk)
    y1, h2, comb = _outproj(ao, go, x, p["woa"], p["wog"], p["g2"], p["w_rh"], p["w_rl"], p["b_r"])
    y = _moe(h2, y1, comb, p["wgu"], p["wd"])
    return y, k_out, v_out, jnp.swapaxes(st, -1, -2)


def kernel(x_prompt, x_sample, cache_k, cache_v, state_gla, norm1_gain, w_in, a_q_gain, a_k_gain, lambda_q1, lambda_k1, lambda_q2, lambda_k2, a_out_gain, w_a2, b_a, g_out_gain, w_out, norm2_gain, w_group, b_group, w_erouter, b_erouter, w_gate, w_up, w_down):
    weights = (norm1_gain, w_in, a_q_gain, a_k_gain, lambda_q1, lambda_k1, lambda_q2, lambda_k2, a_out_gain,
               w_a2, b_a, g_out_gain, w_out, norm2_gain, w_group, b_group, w_erouter, b_erouter, w_gate,
               w_up, w_down)
    depth = w_in.shape[0]
    pb, pl_, d = x_prompt.shape
    sb, sl, _ = x_sample.shape
    past = cache_k.shape[2]
    y_p = x_prompt.reshape(pb * pl_, d)
    y_s = x_sample.reshape(sb * sl, d)
    outs = [[] for _ in range(6)]
    for l in range(depth):
        lam_init = 0.8 - 0.6 * math.exp(-0.3 * l)
        p = _layer_weights(l, *weights)
        y_p, kp, vp, sp = _mix_and_moe(y_p, p, batch=pb, seq=pl_, chunk=CHUNK, lam_init=lam_init)
        cache = (jnp.transpose(cache_k[l], (0, 2, 3, 4, 1)).reshape(sb, A_WIDTH, past),
                 cache_v[l].reshape(sb * past * A_HEADS, A_DV))
        y_s, kn, vn, sn = _mix_and_moe(y_s, p, batch=sb, seq=sl, chunk=sl, lam_init=lam_init,
                                       cache=cache, state=state_gla[l])
        for o, t in zip(outs, (kp, vp, sp, kn, vn, sn)):
            o.append(t)
    return (y_p.reshape(pb, pl_, d), y_s.reshape(sb, sl, d)) + tuple(jnp.stack(o) for o in outs)
```

```python
import functools
import math

import jax
import jax.numpy as jnp
from jax import lax
from jax.experimental import pallas as pl
from jax.experimental.pallas import tpu as pltpu

F32 = jnp.float32
BF16 = jnp.bfloat16

D_MODEL = 1024
CHUNK = 64
A_HEADS = 4
A_DH = 64
A_DV = 128
A_WIDTH = A_HEADS * A_DV
G_HEADS = 4
G_DK = 64
G_DV = 128
G_WIDTH = G_HEADS * G_DV
G_RANK = 16
G_TAU = 16.0
N_GROUPS = 4
EXPERTS_PER_GROUP = 8
N_EXPERTS = N_GROUPS * EXPERTS_PER_GROUP
D_EXPERT = D_MODEL // 4
EPS = 1e-6

LANES = 128
NEG = -1e30
VMEM_LIMIT = 56 * 1024 * 1024

_C_AQ, _C_AK, _C_AV = 0, 512, 1024
_C_GQ, _C_GK, _C_GV = 1536, 1792, 2048
_C_GA, _C_GR = 2560, 2576
W_BIG = 3328

INPROJ_TM = 512
ATT_TQ = 512
ATT_TK = INPROJ_TM
ATT_RG = 256
LOG2E = 1.4426950408889634
ATT_SAFE_BOUND = 43.0 * LOG2E
ROUTE_OFF = N_GROUPS
TOK_ROWS = D_MODEL // LANES
MOE_TM = 1024
MOE_SLOT_TILE = 256


def _cparams(sem):
    return pltpu.CompilerParams(dimension_semantics=sem, vmem_limit_bytes=VMEM_LIMIT)


def _nt(a, b):
    return lax.dot_general(a, b, (((1,), (1,)), ((), ())), preferred_element_type=F32)


def _tn(a, b):
    return lax.dot_general(a, b, (((0,), (0,)), ((), ())), preferred_element_type=F32)


def _dot(a, b):
    return jnp.dot(a, b, preferred_element_type=F32)


def _fold_kernel(wga_ref, wa2_ref, out_ref):
    out_ref[...] = jnp.dot(wga_ref[...], wa2_ref[...], preferred_element_type=F32,
                           precision=lax.Precision.HIGHEST)


def _fold(w_ga, w_a2):
    return pl.pallas_call(
        _fold_kernel,
        out_shape=jax.ShapeDtypeStruct((D_MODEL, G_HEADS * G_DK), F32),
        name="fold_gate",
    )(w_ga, w_a2)


def _headnorm(z, gain2):
    outs = []
    lane = lax.broadcasted_iota(jnp.int32, (z.shape[0], LANES), 1)
    lo = lane < A_DH
    for c in range(z.shape[1] // LANES):
        x = z[:, c * LANES:(c + 1) * LANES]
        xx = x * x
        s_lo = jnp.sum(jnp.where(lo, xx, 0.0), axis=-1, keepdims=True)
        s_hi = jnp.sum(jnp.where(lo, 0.0, xx), axis=-1, keepdims=True)
        r = jnp.where(lo, lax.rsqrt(s_lo * (1.0 / A_DH) + EPS), lax.rsqrt(s_hi * (1.0 / A_DH) + EPS))
        outs.append((x * r) * gain2)
    return outs


def _inproj_kernel(x_ref, g1_ref, w_ref, qg_ref, kg_ref, ba_ref, *out_refs, final_layout):
    if final_layout:
        q_ref, kt_ref, ktb_ref, v4_ref, vb_ref, qk_ref, gv_ref, gr_ref, la_ref = out_refs
    else:
        q_ref, k_ref, v_ref, qk_ref, gv_ref, gr_ref, la_ref = out_refs
    x = x_ref[...]
    tm = x.shape[0]
    ms = jnp.mean(x * x, axis=-1, keepdims=True)
    h = ((x * lax.rsqrt(ms + EPS)) * g1_ref[...]).astype(BF16)

    def seg(lo, hi):
        return _dot(h, w_ref[:, lo:hi])

    for c, y in enumerate(_headnorm(seg(0, 512), qg_ref[...])):
        q_ref[:, c * LANES:(c + 1) * LANES] = (y * (A_DH ** -0.5 * LOG2E)).astype(BF16)
    for c, y in enumerate(_headnorm(seg(512, 1024), kg_ref[...])):
        cols = slice(c * LANES, (c + 1) * LANES)
        if final_layout:
            yt = y.T
            kt_ref[cols, :] = yt
            ktb_ref[cols, :] = yt.astype(BF16)
        else:
            k_ref[:, cols] = y
    v = seg(1024, 1536)
    if final_layout:
        vb_ref[...] = v.astype(BF16)
        for c in range(A_HEADS):
            v4_ref[pl.ds(c, tm, stride=A_HEADS), :] = v[:, c * LANES:(c + 1) * LANES]
    else:
        v_ref[...] = v
    qk_ref[...] = seg(1536, 2048)
    gv_ref[...] = seg(2048, 2560)
    gr_ref[...] = seg(2560, 3072)
    zl = seg(3072, 3328) + ba_ref[...]
    la_ref[...] = (jnp.minimum(zl, 0.0) - jnp.log1p(jnp.exp(-jnp.abs(zl)))) * (1.0 / G_TAU)


def _inproj(x, g1, w_big, qg2, kg2, ba, *, batch, seq, final_layout):
    n = x.shape[0]
    tm = min(INPROJ_TM, seq)
    nlt = seq // tm
    row = lambda w: pl.BlockSpec((tm, w), lambda i: (i, 0))
    full = lambda a: pl.BlockSpec(a.shape, lambda i: (0, 0))
    tail = [(row(512), (n, 512), F32)] * 3 + [(row(256), (n, 256), F32)]
    if final_layout:
        outs = [
            (row(512), (n, 512), BF16),
            (pl.BlockSpec((None, 512, tm), lambda i: (i // nlt, 0, i % nlt)), (batch, 512, seq), F32),
            (pl.BlockSpec((None, None, 512, tm), lambda i: (i // nlt, i % nlt, 0, 0)),
             (batch, nlt, 512, tm), BF16),
            (pl.BlockSpec((tm * A_HEADS, LANES), lambda i: (i, 0)), (n * A_HEADS, LANES), F32),
            (row(512), (n, 512), BF16),
        ] + tail
    else:
        outs = [(row(512), (n, 512), BF16), (row(512), (n, 512), F32), (row(512), (n, 512), F32)] + tail
    return pl.pallas_call(
        functools.partial(_inproj_kernel, final_layout=final_layout),
        grid=(n // tm,),
        in_specs=[row(D_MODEL), full(g1), full(w_big), full(qg2), full(kg2), full(ba)],
        out_specs=[o[0] for o in outs],
        out_shape=[jax.ShapeDtypeStruct(o[1], o[2]) for o in outs],
        compiler_params=_cparams(("parallel",)),
        name="inproj",
    )(x, g1, w_big, qg2, kg2, ba)


def _diff_lambda(lq1_ref, lk1_ref, lq2_ref, lk2_ref, lam_init):
    a = jnp.sum(lq1_ref[...] * lk1_ref[...], axis=-1, keepdims=True)
    b = jnp.sum(lq2_ref[...] * lk2_ref[...], axis=-1, keepdims=True)
    return jnp.exp(a) - jnp.exp(b) + lam_init


def _attn_prompt_kernel(lq1_ref, lk1_ref, lq2_ref, lk2_ref, gain_ref, q_ref, kt_ref, v_ref,
                        o_ref, acc_s, *, seq, lam_init):
    tq, tk, rg = ATT_TQ, ATT_TK, ATT_RG
    nrg = 2 * tq // rg
    lam = _diff_lambda(lq1_ref, lk1_ref, lq2_ref, lk2_ref, lam_init)
    lane = lax.broadcasted_iota(jnp.int32, (tq, LANES), 1)
    lo = lane < A_DH
    ones = jnp.ones((tk, LANES), BF16)
    qq = lax.broadcasted_iota(jnp.int32, (rg, tk), 0)
    kk = lax.broadcasted_iota(jnp.int32, (rg, tk), 1)
    visible = [(kk // CHUNK) <= ((qq + part * rg) // CHUNK) for part in range(tq // rg)]
    unit = jnp.where(lax.broadcasted_iota(jnp.int32, (LANES, tk), 0) == 0, 1.0, 0.0).astype(BF16)

    def key_norms(j, c):
        kb = kt_ref[j].astype(F32)
        sq = kb * kb
        n1 = jnp.max(jnp.sum(sq[:A_DH], axis=0, keepdims=True), axis=1, keepdims=True)
        n2 = jnp.max(jnp.sum(sq[A_DH:], axis=0, keepdims=True), axis=1, keepdims=True)
        return jnp.maximum(c[0], n1), jnp.maximum(c[1], n2)

    zero11 = jnp.zeros((1, 1), F32)
    k1sq, k2sq = lax.fori_loop(0, seq // tk, key_norms, (zero11, zero11))

    def values(j):
        vb = v_ref[pl.ds(pl.multiple_of(j * tk, tk), tk), :]
        return jnp.concatenate([vb, ones], axis=1)

    def shifted_step(j, carry, qx, diagonal):
        ktx = jnp.concatenate([kt_ref[j], unit], axis=0)
        vx = values(j)
        for g in range(nrg):
            rows = slice(g * rg, (g + 1) * rg)
            p = jnp.exp2(_dot(qx[rows], ktx))
            if diagonal:
                p = jnp.where(visible[g % (tq // rg)], p, 0.0)
            acc_s[rows, :] += _dot(p.astype(BF16), vx)
        return carry

    def running_max_step(j, ms, qz, diagonal):
        kt = kt_ref[j]
        vx = values(j)
        out = []
        for g in range(nrg):
            rows = slice(g * rg, (g + 1) * rg)
            s = _dot(qz[rows], kt)
            if diagonal:
                s = jnp.where(visible[g % (tq // rg)], s, NEG)
            m_new = jnp.maximum(ms[g], jnp.max(s, axis=-1, keepdims=True))
            alpha = jnp.exp2(ms[g] - m_new)
            p = jnp.exp2(s - m_new).astype(BF16)
            acc_s[rows, :] = alpha * acc_s[rows, :] + _dot(p, vx)
            out.append(m_new)
        return tuple(out)

    def q_block(i, carry):
        qrows = pl.ds(pl.multiple_of(i * tq, tq), tq)
        qi = q_ref[qrows, :]
        zero = jnp.zeros_like(qi)
        q1 = jnp.where(lo, qi, zero)
        q2 = jnp.where(lo, zero, qi)
        qf = qi.astype(F32)
        sq = qf * qf
        shift1 = jnp.sqrt(jnp.sum(jnp.where(lo, sq, 0.0), axis=-1, keepdims=True) * k1sq)
        shift2 = jnp.sqrt(jnp.sum(jnp.where(lo, 0.0, sq), axis=-1, keepdims=True) * k2sq)
        safe = jnp.max(jnp.maximum(shift1, shift2)) <= ATT_SAFE_BOUND
        acc_s[...] = jnp.zeros_like(acc_s)

        @pl.when(safe)
        def _():
            x1 = jnp.where(lane == 0, -shift1, 0.0).astype(BF16)
            x2 = jnp.where(lane == 0, -shift2, 0.0).astype(BF16)
            qx = jnp.concatenate([jnp.concatenate([q1, x1], axis=1), jnp.concatenate([q2, x2], axis=1)], axis=0)
            lax.fori_loop(0, i, lambda j, c: shifted_step(j, c, qx, False), 0)
            shifted_step(i, 0, qx, True)

        @pl.when(jnp.logical_not(safe))
        def _():
            qz = jnp.concatenate([q1, q2], axis=0)
            ms = (jnp.full((rg, 1), NEG, F32),) * nrg
            ms = lax.fori_loop(0, i, lambda j, c: running_max_step(j, c, qz, False), ms)
            running_max_step(i, ms, qz, True)

        a1 = acc_s[:tq, :]
        a2 = acc_s[tq:, :]
        o = a1[:, :A_DV] / a1[:, A_DV:] - lam * (a2[:, :A_DV] / a2[:, A_DV:])
        msq = jnp.mean(o * o, axis=-1, keepdims=True)
        o_ref[qrows, :] = (((o * lax.rsqrt(msq + EPS)) * gain_ref[...]) * (1.0 - lam_init)).astype(o_ref.dtype)
        return carry

    lax.fori_loop(0, seq // tq, q_block, 0)


def _attn_prompt(lams, gain_row, q, ktb, v, *, batch, seq, lam_init):
    nkv = seq // ATT_TK
    vec = pl.BlockSpec((1, A_DH), lambda b, h: (0, 0))
    head = pl.BlockSpec((seq, LANES), lambda b, h: (b, h))
    return pl.pallas_call(
        functools.partial(_attn_prompt_kernel, seq=seq, lam_init=lam_init),
        grid=(batch, A_HEADS),
        in_specs=[vec, vec, vec, vec, pl.BlockSpec((1, A_DV), lambda b, h: (0, 0)), head,
                  pl.BlockSpec((None, nkv, LANES, ATT_TK), lambda b, h: (b, 0, h, 0)), head],
        out_specs=head,
        out_shape=jax.ShapeDtypeStruct((batch * seq, A_WIDTH), BF16),
        scratch_shapes=[pltpu.VMEM((2 * ATT_TQ, A_DV + LANES), F32)],
        compiler_params=_cparams(("parallel", "parallel")),
        name="attn_prompt",
    )(*lams, gain_row, q, ktb, v)


def _attn_sample_kernel(lq1_ref, lk1_ref, lq2_ref, lk2_ref, gain_ref, q_ref, kn_ref, vn_ref,
                        kct_ref, vc_ref, o_ref, *, past, lam_init):
    lam = _diff_lambda(lq1_ref, lk1_ref, lq2_ref, lk2_ref, lam_init)
    q = q_ref[...]
    nq = q.shape[0]
    lane = lax.broadcasted_iota(jnp.int32, q.shape, 1)
    zero = jnp.zeros_like(q)
    qz = jnp.concatenate([jnp.where(lane < A_DH, q, zero), jnp.where(lane < A_DH, zero, q)], axis=0)
    vc = vc_ref[pl.ds(pl.program_id(1), past, stride=A_HEADS), :].astype(BF16)
    sc = _dot(qz, kct_ref[...].astype(BF16))
    sn = _nt(qz, kn_ref[...].astype(BF16))
    m = jnp.maximum(jnp.max(sc, axis=-1, keepdims=True), jnp.max(sn, axis=-1, keepdims=True))
    ec = jnp.exp2(sc - m)
    en = jnp.exp2(sn - m)
    l = jnp.sum(ec, axis=-1, keepdims=True) + jnp.sum(en, axis=-1, keepdims=True)
    pv = _dot(ec.astype(BF16), vc) + _dot(en.astype(BF16), vn_ref[...].astype(BF16))
    pv = pv * (1.0 / l)
    o = pv[:nq] - lam * pv[nq:]
    ms = jnp.mean(o * o, axis=-1, keepdims=True)
    o_ref[...] = (((o * lax.rsqrt(ms + EPS)) * gain_ref[...]) * (1.0 - lam_init)).astype(o_ref.dtype)


def _attn_sample(lams, gain_row, q, k, v, kct, vc, *, batch, seq, past, lam_init):
    vec = pl.BlockSpec((1, A_DH), lambda b, h: (0, 0))
    new = pl.BlockSpec((seq, LANES), lambda b, h: (b, h))
    return pl.pallas_call(
        functools.partial(_attn_sample_kernel, past=past, lam_init=lam_init),
        grid=(batch, A_HEADS),
        in_specs=[vec, vec, vec, vec, pl.BlockSpec((1, A_DV), lambda b, h: (0, 0)), new, new, new,
                  pl.BlockSpec((None, LANES, past), lambda b, h: (b, h, 0)),
                  pl.BlockSpec((past * A_HEADS, LANES), lambda b, h: (b, 0))],
        out_specs=new,
        out_shape=jax.ShapeDtypeStruct((batch * seq, A_WIDTH), BF16),
        compiler_params=_cparams(("parallel", "arbitrary")),
        name="attn_sample",
    )(*lams, gain_row, q, k, v, kct, vc)


def _gla_kernel(qk_ref, la_ref, v_ref, gr_ref, gain_ref, s0_ref, go_ref, st_ref, st_s, *, chunk, rows_per_step):
    c = chunk

    @pl.when(pl.program_id(1) == 0)
    def _():
        st_s[...] = s0_ref[...]

    lane = lax.broadcasted_iota(jnp.int32, (c, LANES), 1)
    lo = lane < G_DK
    rr = lax.broadcasted_iota(jnp.int32, (c, c), 0)
    cc = lax.broadcasted_iota(jnp.int32, (c, c), 1)
    causal = rr >= cc
    tril = jnp.where(causal, 1.0, 0.0).astype(BF16)
    scale = G_DK ** -0.5

    def chunk_step(ci, carry):
        rows = pl.ds(pl.multiple_of(ci * c, c), c)
        for h in range(G_HEADS):
            cols = slice(h * LANES, (h + 1) * LANES)
            pair = slice((h // 2) * LANES, (h // 2 + 1) * LANES)
            qk = qk_ref[rows, cols]
            kq = pltpu.roll(qk, G_DK, 1)
            lag = la_ref[rows, pair]
            lar = pltpu.roll(lag, G_DK, 1)
            la2 = jnp.where(lo, lag, lar) if h % 2 == 0 else jnp.where(lo, lar, lag)
            la_hi = la2.astype(BF16)
            la_lo = (la2 - la_hi.astype(F32)).astype(BF16)
            b = _dot(tril, la_hi) + _dot(tril, la_lo)
            b_last = b[c - 1:c, :]
            b_mid = b[c // 2 - 1:c // 2, :]
            qt = (qk * jnp.exp(b - b_mid)) * scale
            kt = kq * jnp.exp(b_mid - b)
            qe = (qk * jnp.exp(b)) * scale
            kh = kq * jnp.exp(b_last - b)
            sc = jnp.where(causal, _nt(qt[:, :G_DK].astype(BF16), kt[:, :G_DK].astype(BF16)), 0.0)
            vb = v_ref[rows, cols].astype(BF16)
            st = st_s[h]
            o = _dot(sc.astype(BF16), vb) + _nt(qe[:, :G_DK].astype(BF16), st.astype(BF16))
            st_s[h] = jnp.exp(b_last)[:, :G_DK] * st + _tn(vb, kh[:, :G_DK].astype(BF16))
            ms = jnp.mean(o * o, axis=-1, keepdims=True)
            on = (o * lax.rsqrt(ms + EPS)) * gain_ref[...]
            g = gr_ref[rows, cols]
            go_ref[rows, cols] = (on * (g * jax.nn.sigmoid(g))).astype(go_ref.dtype)
        return carry

    lax.fori_loop(0, rows_per_step // c, chunk_step, 0)
    st_ref[...] = st_s[...]


def _gla(qk, la, v, gr, gain_row, s0t, *, batch, seq, chunk):
    lb = min(seq, 1024)
    nl = seq // lb
    wide = pl.BlockSpec((lb, G_WIDTH), lambda b, l: (b * nl + l, 0))
    state = pl.BlockSpec((None, G_HEADS, G_DV, G_DK), lambda b, l: (b, 0, 0, 0))
    return pl.pallas_call(
        functools.partial(_gla_kernel, chunk=chunk, rows_per_step=lb),
        grid=(batch, nl),
        in_specs=[wide, pl.BlockSpec((lb, G_HEADS * G_DK), lambda b, l: (b * nl + l, 0)), wide, wide,
                  pl.BlockSpec((1, G_DV), lambda b, l: (0, 0)), state],
        out_specs=[wide, state],
        out_shape=[jax.ShapeDtypeStruct((batch * seq, G_WIDTH), BF16),
                   jax.ShapeDtypeStruct((batch, G_HEADS, G_DV, G_DK), F32)],
        scratch_shapes=[pltpu.VMEM((G_HEADS, G_DV, G_DK), F32)],
        compiler_params=_cparams(("arbitrary", "arbitrary")),
        name="gla",
    )(qk, la, v, gr, gain_row, s0t)


R_E1, R_E2, R_W1, R_W2, R_RANK1, R_RANK2, R_FIELDS = 0, 1, 2, 3, 4, 5, 8


def _route(z, prefix_of):
    lane = lax.broadcasted_iota(jnp.int32, z.shape, 1)
    big = jnp.int32(LANES)

    def first_argmax(vals, vmax):
        return jnp.min(jnp.where(vals == vmax, lane, big), axis=-1, keepdims=True)

    zg = jnp.where(lane < N_GROUPS, z, NEG)
    gmax = jnp.max(zg, axis=-1, keepdims=True)
    g_idx = first_argmax(zg, gmax)
    g_w = 1.0 / jnp.sum(jnp.exp(zg - gmax), axis=-1, keepdims=True)
    first = ROUTE_OFF + EXPERTS_PER_GROUP * g_idx
    ze = jnp.where(lane < first, NEG, jnp.where(lane < first + EXPERTS_PER_GROUP, z, NEG))
    v1 = jnp.max(ze, axis=-1, keepdims=True)
    i1 = first_argmax(ze, v1)
    ze2 = jnp.where(lane == i1, NEG, ze)
    v2 = jnp.max(ze2, axis=-1, keepdims=True)
    i2 = first_argmax(ze2, v2)
    t = jnp.exp(v2 - v1)
    w1 = g_w / (1.0 + t)
    w2 = w1 * t
    hot1 = lane == i1
    hot2 = lane == i2
    one_hot = jnp.where(hot1, 1.0, jnp.where(hot2, 1.0, 0.0))
    before = prefix_of(one_hot)
    rank1 = jnp.sum(jnp.where(hot1, before, 0.0), axis=-1, keepdims=True)
    rank2 = jnp.sum(jnp.where(hot2, before, 0.0), axis=-1, keepdims=True)
    fields = ((R_E1, (i1 - ROUTE_OFF).astype(F32)), (R_E2, (i2 - ROUTE_OFF).astype(F32)), (R_W1, w1), (R_W2, w2),
              (R_RANK1, rank1), (R_RANK2, rank2))
    rec = jnp.zeros(z.shape, F32)
    for pos, val in fields:
        rec = jnp.where(lane == pos, val, rec)
    return rec, one_hot


def _outproj_kernel(ao_ref, go_ref, x_ref, woa_ref, wog_ref, g2_ref, wrh_ref, wrl_ref, br_ref,
                    y1_ref, h2_ref, rec_ref, rect_ref, cnt_ref, cnt_s):
    tm = x_ref.shape[0]

    @pl.when(pl.program_id(0) == 0)
    def _():
        cnt_s[...] = jnp.zeros_like(cnt_s)

    y1 = x_ref[...] + _dot(ao_ref[...], woa_ref[...]) + _dot(go_ref[...], wog_ref[...])
    y1_ref[...] = y1
    ms = jnp.mean(y1 * y1, axis=-1, keepdims=True)
    h2 = (y1 * lax.rsqrt(ms + EPS)) * g2_ref[...]
    hh = h2.astype(BF16)
    hb = hh.astype(F32)
    for c in range(TOK_ROWS):
        h2_ref[pl.ds(c, tm, stride=TOK_ROWS), :] = hb[:, c * LANES:(c + 1) * LANES]
    hl = (h2 - hb).astype(BF16)
    z = _dot(hh, wrh_ref[...]) + _dot(hl, wrh_ref[...]) + _dot(hh, wrl_ref[...]) + br_ref[...]

    rr = lax.broadcasted_iota(jnp.int32, (tm, tm), 0)
    cc = lax.broadcasted_iota(jnp.int32, (tm, tm), 1)
    earlier = jnp.where(cc < rr, 1.0, 0.0).astype(BF16)

    def prefix_of(one_hot):
        return _dot(earlier, one_hot.astype(BF16)) + cnt_s[...]

    rec, one_hot = _route(z, prefix_of)
    rec_ref[...] = rec
    rect_ref[...] = rec.T[:R_FIELDS, :]
    cnt_s[...] += jnp.sum(one_hot, axis=0, keepdims=True)
    cnt_ref[...] = cnt_s[...]


def _outproj(ao, go, x, woa, wog, g2, wrh, wrl, br):
    n = x.shape[0]
    tm = min(512, n)
    row = lambda w: pl.BlockSpec((tm, w), lambda i: (i, 0))
    full = lambda a: pl.BlockSpec(a.shape, lambda i: (0, 0))
    return pl.pallas_call(
        _outproj_kernel,
        grid=(n // tm,),
        in_specs=[row(A_WIDTH), row(G_WIDTH), row(D_MODEL), full(woa), full(wog), full(g2),
                  full(wrh), full(wrl), full(br)],
        out_specs=[row(D_MODEL), pl.BlockSpec((tm * TOK_ROWS, LANES), lambda i: (i, 0)), row(LANES),
                   pl.BlockSpec((R_FIELDS, tm), lambda i: (0, i)), pl.BlockSpec((1, LANES), lambda i: (0, 0))],
        out_shape=[jax.ShapeDtypeStruct((n, D_MODEL), F32), jax.ShapeDtypeStruct((n * TOK_ROWS, LANES), F32),
                   jax.ShapeDtypeStruct((n, LANES), F32), jax.ShapeDtypeStruct((R_FIELDS, n), F32),
                   jax.ShapeDtypeStruct((1, LANES), F32)],
        scratch_shapes=[pltpu.VMEM((1, LANES), F32)],
        compiler_params=_cparams(("arbitrary",)),
        name="outproj_router",
    )(ao, go, x, woa, wog, g2, wrh, wrl, br)


def _token_rows(ref, t):
    return ref.at[pl.ds(pl.multiple_of(t * TOK_ROWS, TOK_ROWS), TOK_ROWS), :]


def _dispatch_kernel(pos1_ref, pos2_ref, h_ref, xs_ref, sem):
    tm = h_ref.shape[0] // TOK_ROWS

    def copies(t):
        src = _token_rows(h_ref, t)
        return [pltpu.make_async_copy(src, _token_rows(xs_ref, p[0, t]), sem) for p in (pos1_ref, pos2_ref)]

    def start(t, c):
        for cp in copies(t):
            cp.start()
        return c

    def wait(t, c):
        for cp in copies(t):
            cp.wait()
        return c

    lax.fori_loop(0, tm, start, 0)
    lax.fori_loop(0, tm, wait, 0)


def _dispatch(pos1, pos2, h2t, *, tm):
    n = h2t.shape[0] // TOK_ROWS
    idx = pl.BlockSpec((None, 1, tm), lambda i: (i, 0, 0), memory_space=pltpu.SMEM)
    return pl.pallas_call(
        _dispatch_kernel,
        grid=(n // tm,),
        in_specs=[idx, idx, pl.BlockSpec((tm * TOK_ROWS, LANES), lambda i: (i, 0))],
        out_specs=pl.BlockSpec(memory_space=pl.ANY),
        out_shape=jax.ShapeDtypeStruct((2 * n * TOK_ROWS, LANES), F32),
        scratch_shapes=[pltpu.SemaphoreType.DMA(())],
        compiler_params=_cparams(("arbitrary",)),
        name="moe_dispatch",
    )(pos1, pos2, h2t)


def _experts_kernel(vt_ref, ve_ref, vlo_ref, vhi_ref, vfirst_ref, xs_ref, wgu_ref, wd_ref, ys_ref):
    v = pl.program_id(0)
    t = xs_ref.shape[0] // TOK_ROWS
    x = jnp.concatenate([xs_ref[pl.ds(c, t, stride=TOK_ROWS), :] for c in range(TOK_ROWS)], axis=1)
    gu = _dot(x.astype(BF16), wgu_ref[...])
    g = gu[:, :D_EXPERT]
    u = gu[:, D_EXPERT:]
    a = ((g * jax.nn.sigmoid(g)) * u).astype(BF16)
    d = _dot(a, wd_ref[...])
    row = lax.broadcasted_iota(jnp.int32, (t, LANES), 0)
    mine = jnp.logical_and(row >= vlo_ref[v], row < vhi_ref[v])

    @pl.when(vfirst_ref[v] == 1)
    def _():
        for c in range(TOK_ROWS):
            ys_ref[pl.ds(c, t, stride=TOK_ROWS), :] = jnp.where(mine, d[:, c * LANES:(c + 1) * LANES], 0.0)

    @pl.when(vfirst_ref[v] == 0)
    def _():
        for c in range(TOK_ROWS):
            rows = pl.ds(c, t, stride=TOK_ROWS)
            ys_ref[rows, :] = jnp.where(mine, d[:, c * LANES:(c + 1) * LANES], ys_ref[rows, :])


def _experts(visits, xs, wgu, wd, *, t):
    nv = visits[0].shape[0]
    slot = pl.BlockSpec((t * TOK_ROWS, LANES), lambda v, vt, ve, lo, hi, fi: (vt[v], 0))
    return pl.pallas_call(
        _experts_kernel,
        grid_spec=pltpu.PrefetchScalarGridSpec(
            num_scalar_prefetch=5,
            grid=(nv,),
            in_specs=[slot,
                      pl.BlockSpec((None, D_MODEL, 2 * D_EXPERT), lambda v, vt, ve, lo, hi, fi: (ve[v], 0, 0)),
                      pl.BlockSpec((None, D_EXPERT, D_MODEL), lambda v, vt, ve, lo, hi, fi: (ve[v], 0, 0))],
            out_specs=slot,
        ),
        out_shape=jax.ShapeDtypeStruct(xs.shape, F32),
        compiler_params=_cparams(("arbitrary",)),
        name="moe_experts",
    )(*visits, xs, wgu, wd)


def _combine_kernel(pos1_ref, pos2_ref, y1_ref, rec_ref, ys_ref, out_ref, g1_s, g2_s, sem):
    tm = y1_ref.shape[0]

    def copies(t):
        return [pltpu.make_async_copy(_token_rows(ys_ref, p[0, t]), _token_rows(g, t), sem)
                for p, g in ((pos1_ref, g1_s), (pos2_ref, g2_s))]

    def start(t, c):
        for cp in copies(t):
            cp.start()
        return c

    def wait(t, c):
        for cp in copies(t):
            cp.wait()
        return c

    lax.fori_loop(0, tm, start, 0)
    rec = rec_ref[...]
    w1 = rec[:, R_W1:R_W1 + 1]
    w2 = rec[:, R_W2:R_W2 + 1]
    lax.fori_loop(0, tm, wait, 0)
    for c in range(TOK_ROWS):
        rows = pl.ds(c, tm, stride=TOK_ROWS)
        cols = slice(c * LANES, (c + 1) * LANES)
        out_ref[:, cols] = y1_ref[:, cols] + w1 * g1_s[rows, :] + w2 * g2_s[rows, :]


def _combine(pos1, pos2, y1, rec, ys, *, tm):
    n = y1.shape[0]
    idx = pl.BlockSpec((None, 1, tm), lambda i: (i, 0, 0), memory_space=pltpu.SMEM)
    row = lambda w: pl.BlockSpec((tm, w), lambda i: (i, 0))
    return pl.pallas_call(
        _combine_kernel,
        grid=(n // tm,),
        in_specs=[idx, idx, row(D_MODEL), row(LANES), pl.BlockSpec(memory_space=pl.ANY)],
        out_specs=row(D_MODEL),
        out_shape=jax.ShapeDtypeStruct((n, D_MODEL), F32),
        scratch_shapes=[pltpu.VMEM((tm * TOK_ROWS, LANES), F32), pltpu.VMEM((tm * TOK_ROWS, LANES), F32),
                        pltpu.SemaphoreType.DMA(())],
        compiler_params=_cparams(("arbitrary",)),
        name="moe_combine",
    )(pos1, pos2, y1, rec, ys)


def _expert_visits(counts, n_slots, t):
    n_tiles = n_slots // t
    nv = n_tiles + N_EXPERTS - 1
    end = jnp.cumsum(counts)
    start = end - counts
    first_tile = start // t
    last_tile = jnp.maximum(end - 1, start) // t
    n_vis = jnp.where(counts > 0, last_tile - first_tile + 1, 0)
    vis_end = jnp.cumsum(n_vis)
    total = vis_end[-1]
    v = jnp.arange(nv, dtype=jnp.int32)
    e = jnp.minimum(jnp.sum((vis_end[None, :] <= v[:, None]).astype(jnp.int32), axis=1), N_EXPERTS - 1)
    pick = lambda a: jnp.sum(jnp.where(e[:, None] == jnp.arange(N_EXPERTS)[None, :], a[None, :], 0), axis=1)
    tile = pick(first_tile) + (v - (pick(vis_end) - pick(n_vis)))
    lo = jnp.maximum(pick(start), tile * t) - tile * t
    hi = jnp.minimum(pick(end), (tile + 1) * t) - tile * t
    live = v < total
    last_e = jnp.max(jnp.where(counts > 0, jnp.arange(N_EXPERTS), 0))
    tile = jnp.where(live, tile, n_tiles - 1)
    e = jnp.where(live, e, last_e)
    lo = jnp.where(live, lo, 0)
    hi = jnp.where(live, hi, 0)
    first = jnp.concatenate([jnp.ones((1,), jnp.int32), (tile[1:] != tile[:-1]).astype(jnp.int32)])
    return tuple(a.astype(jnp.int32) for a in (tile, e, lo, hi, first))


def _moe(h2t, y1, rec, rect, cnt, wgu, wd):
    n = y1.shape[0]
    tm = min(MOE_TM, n)
    t = min(MOE_SLOT_TILE, 2 * n)
    counts = cnt[0, ROUTE_OFF:ROUTE_OFF + N_EXPERTS].astype(jnp.int32)
    start = jnp.cumsum(counts) - counts
    ids = rect.astype(jnp.int32)
    experts = jnp.arange(N_EXPERTS, dtype=jnp.int32)[:, None]

    def slot(e_row, rank_row):
        return (jnp.sum(jnp.where(ids[e_row][None, :] == experts, start[:, None], 0), axis=0)
                + ids[rank_row]).reshape(n // tm, 1, tm)

    pos1 = slot(R_E1, R_RANK1)
    pos2 = slot(R_E2, R_RANK2)
    xs = _dispatch(pos1, pos2, h2t, tm=tm)
    ys = _experts(_expert_visits(counts, 2 * n, t), xs, wgu, wd, t=t)
    return _combine(pos1, pos2, y1, rec, ys, tm=tm)


def _layer_weights(l, norm1_gain, w_in, a_q_gain, a_k_gain, lambda_q1, lambda_k1, lambda_q2, lambda_k2,
                   a_out_gain, w_a2, b_a, g_out_gain, w_out, norm2_gain, w_group, b_group, w_erouter,
                   b_erouter, w_gate, w_up, w_down):
    w = w_in[l]
    w_la = _fold(w[:, _C_GA:_C_GR], w_a2[l])
    qk_cols = []
    for h in range(G_HEADS):
        qk_cols += [w[:, _C_GQ + h * G_DK:_C_GQ + (h + 1) * G_DK], w[:, _C_GK + h * G_DK:_C_GK + (h + 1) * G_DK]]
    w_big = jnp.concatenate([w[:, :_C_GQ]] + qk_cols + [w[:, _C_GV:_C_GA], w[:, _C_GR:], w_la], axis=1).astype(BF16)
    w_r = jnp.concatenate([w_group[l], w_erouter[l],
                           jnp.zeros((D_MODEL, LANES - N_GROUPS - N_EXPERTS), F32)], axis=1)
    w_rh = w_r.astype(BF16)
    w_rl = (w_r - w_rh.astype(F32)).astype(BF16)
    b_r = jnp.concatenate([b_group[l], b_erouter[l], jnp.zeros((LANES - N_GROUPS - N_EXPERTS,), F32)])[None, :]
    return dict(
        g1=norm1_gain[l][None, :], w_big=w_big,
        qg2=jnp.tile(a_q_gain[l], 2)[None, :], kg2=jnp.tile(a_k_gain[l], 2)[None, :],
        ba=b_a[l][None, :],
        lams=(lambda_q1[l][None, :], lambda_k1[l][None, :], lambda_q2[l][None, :], lambda_k2[l][None, :]),
        a_gain=a_out_gain[l][None, :], g_gain=g_out_gain[l][None, :],
        woa=w_out[l][:A_WIDTH].astype(BF16), wog=w_out[l][A_WIDTH:].astype(BF16),
        g2=norm2_gain[l][None, :], w_rh=w_rh, w_rl=w_rl, b_r=b_r,
        wgu=jnp.concatenate([w_gate[l], w_up[l]], axis=-1).astype(BF16), wd=w_down[l].astype(BF16),
    )


def _mix_and_moe(x, p, *, batch, seq, chunk, lam_init, cache=None, state=None):
    prompt = cache is None
    res = _inproj(x, p["g1"], p["w_big"], p["qg2"], p["kg2"], p["ba"], batch=batch, seq=seq, final_layout=prompt)
    if prompt:
        q, kt, ktb, v4, vb, qk, gv, gr, la = res
        ao = _attn_prompt(p["lams"], p["a_gain"], q, ktb, vb, batch=batch, seq=seq, lam_init=lam_init)
        s0t = jnp.zeros((batch, G_HEADS, G_DV, G_DK), F32)
        k_out = jnp.transpose(kt.reshape(batch, A_HEADS, 2, A_DH, seq), (0, 4, 1, 2, 3))
        v_out = v4.reshape(batch, seq, A_HEADS, A_DV)
    else:
        q, k, v, qk, gv, gr, la = res
        kct, vc = cache
        ao = _attn_sample(p["lams"], p["a_gain"], q, k, v, kct, vc, batch=batch, seq=seq,
                          past=kct.shape[-1], lam_init=lam_init)
        s0t = jnp.swapaxes(state, -1, -2)
        k_out = k.reshape(batch, seq, A_HEADS, 2, A_DH)
        v_out = v.reshape(batch, seq, A_HEADS, A_DV)
    go, st = _gla(qk, la, gv, gr, p["g_gain"], s0t, batch=batch, seq=seq, chunk=chunk)
    y1, h2t, rec, rect, cnt = _outproj(ao, go, x, p["woa"], p["wog"], p["g2"], p["w_rh"], p["w_rl"], p["b_r"])
    y = _moe(h2t, y1, rec, rect, cnt, p["wgu"], p["wd"])
    return y, k_out, v_out, jnp.swapaxes(st, -1, -2)


def kernel(x_prompt, x_sample, cache_k, cache_v, state_gla, norm1_gain, w_in, a_q_gain, a_k_gain, lambda_q1, lambda_k1, lambda_q2, lambda_k2, a_out_gain, w_a2, b_a, g_out_gain, w_out, norm2_gain, w_group, b_group, w_erouter, b_erouter, w_gate, w_up, w_down):
    weights = (norm1_gain, w_in, a_q_gain, a_k_gain, lambda_q1, lambda_k1, lambda_q2, lambda_k2, a_out_gain,
               w_a2, b_a, g_out_gain, w_out, norm2_gain, w_group, b_group, w_erouter, b_erouter, w_gate,
               w_up, w_down)
    depth = w_in.shape[0]
    pb, pl_, d = x_prompt.shape
    sb, sl, _ = x_sample.shape
    past = cache_k.shape[2]
    y_p = x_prompt.reshape(pb * pl_, d)
    y_s = x_sample.reshape(sb * sl, d)
    outs = [[] for _ in range(6)]
    for l in range(depth):
        lam_init = 0.8 - 0.6 * math.exp(-0.3 * l)
        p = _layer_weights(l, *weights)
        y_p, kp, vp, sp = _mix_and_moe(y_p, p, batch=pb, seq=pl_, chunk=CHUNK, lam_init=lam_init)
        cache = (jnp.transpose(cache_k[l], (0, 2, 3, 4, 1)).reshape(sb, A_WIDTH, past),
                 cache_v[l].reshape(sb * past * A_HEADS, A_DV))
        y_s, kn, vn, sn = _mix_and_moe(y_s, p, batch=sb, seq=sl, chunk=sl, lam_init=lam_init,
                                       cache=cache, state=state_gla[l])
        for o, t in zip(outs, (kp, vp, sp, kn, vn, sn)):
            o.append(t)
    return (y_p.reshape(pb, pl_, d), y_s.reshape(sb, sl, d)) + tuple(jnp.stack(o) for o in outs)
```

```python
import functools
import math

import jax
import jax.numpy as jnp
from jax import lax
from jax.experimental import pallas as pl
from jax.experimental.pallas import tpu as pltpu

F32 = jnp.float32
BF16 = jnp.bfloat16

D_MODEL = 1024
CHUNK = 64
A_HEADS = 4
A_DH = 64
A_DV = 128
A_WIDTH = A_HEADS * A_DV
G_HEADS = 4
G_DK = 64
G_DV = 128
G_WIDTH = G_HEADS * G_DV
G_RANK = 16
G_TAU = 16.0
N_GROUPS = 4
EXPERTS_PER_GROUP = 8
N_EXPERTS = N_GROUPS * EXPERTS_PER_GROUP
D_EXPERT = D_MODEL // 4
EPS = 1e-6

LANES = 128
NEG = -1e30
VMEM_LIMIT = 56 * 1024 * 1024

_C_AQ, _C_AK, _C_AV = 0, 512, 1024
_C_GQ, _C_GK, _C_GV = 1536, 1792, 2048
_C_GA, _C_GR = 2560, 2576
W_BIG = 3328

INPROJ_TM = 512
ATT_TQ = 512
ATT_TK = INPROJ_TM
ATT_RG = 256
LOG2E = 1.4426950408889634
ATT_SAFE_BOUND = 43.0 * LOG2E
ROUTE_OFF = N_GROUPS
MOE_TM = 1024
MOE_SLOT_TILE = 512
MOE_DMA_UNROLL = 8
MOE_ROW_GROUP = 256


def _cparams(sem):
    return pltpu.CompilerParams(dimension_semantics=sem, vmem_limit_bytes=VMEM_LIMIT)


def _nt(a, b):
    return lax.dot_general(a, b, (((1,), (1,)), ((), ())), preferred_element_type=F32)


def _tn(a, b):
    return lax.dot_general(a, b, (((0,), (0,)), ((), ())), preferred_element_type=F32)


def _dot(a, b):
    return jnp.dot(a, b, preferred_element_type=F32)


def _fold_kernel(wga_ref, wa2_ref, out_ref):
    out_ref[...] = jnp.dot(wga_ref[...], wa2_ref[...], preferred_element_type=F32,
                           precision=lax.Precision.HIGHEST)


def _fold(w_ga, w_a2):
    return pl.pallas_call(
        _fold_kernel,
        out_shape=jax.ShapeDtypeStruct((D_MODEL, G_HEADS * G_DK), F32),
        name="fold_gate",
    )(w_ga, w_a2)


def _headnorm(z, gain2):
    outs = []
    lane = lax.broadcasted_iota(jnp.int32, (z.shape[0], LANES), 1)
    lo = lane < A_DH
    for c in range(z.shape[1] // LANES):
        x = z[:, c * LANES:(c + 1) * LANES]
        xx = x * x
        s_lo = jnp.sum(jnp.where(lo, xx, 0.0), axis=-1, keepdims=True)
        s_hi = jnp.sum(jnp.where(lo, 0.0, xx), axis=-1, keepdims=True)
        r = jnp.where(lo, lax.rsqrt(s_lo * (1.0 / A_DH) + EPS), lax.rsqrt(s_hi * (1.0 / A_DH) + EPS))
        outs.append((x * r) * gain2)
    return outs


def _inproj_kernel(x_ref, g1_ref, w_ref, qg_ref, kg_ref, ba_ref, *out_refs, final_layout):
    if final_layout:
        q_ref, kt_ref, ktb_ref, v4_ref, vb_ref, qk_ref, gv_ref, gr_ref, la_ref = out_refs
    else:
        q_ref, k_ref, v_ref, qk_ref, gv_ref, gr_ref, la_ref = out_refs
    x = x_ref[...]
    tm = x.shape[0]
    ms = jnp.mean(x * x, axis=-1, keepdims=True)
    h = ((x * lax.rsqrt(ms + EPS)) * g1_ref[...]).astype(BF16)

    def seg(lo, hi):
        return _dot(h, w_ref[:, lo:hi])

    for c, y in enumerate(_headnorm(seg(0, 512), qg_ref[...])):
        q_ref[:, c * LANES:(c + 1) * LANES] = (y * (A_DH ** -0.5 * LOG2E)).astype(BF16)
    for c, y in enumerate(_headnorm(seg(512, 1024), kg_ref[...])):
        cols = slice(c * LANES, (c + 1) * LANES)
        if final_layout:
            yt = y.T
            kt_ref[cols, :] = yt
            ktb_ref[cols, :] = yt.astype(BF16)
        else:
            k_ref[:, cols] = y
    v = seg(1024, 1536)
    if final_layout:
        vb_ref[...] = v.astype(BF16)
        for c in range(A_HEADS):
            v4_ref[pl.ds(c, tm, stride=A_HEADS), :] = v[:, c * LANES:(c + 1) * LANES]
    else:
        v_ref[...] = v
    qk_ref[...] = seg(1536, 2048)
    gv_ref[...] = seg(2048, 2560)
    gr_ref[...] = seg(2560, 3072)
    zl = seg(3072, 3328) + ba_ref[...]
    la_ref[...] = (jnp.minimum(zl, 0.0) - jnp.log1p(jnp.exp(-jnp.abs(zl)))) * (1.0 / G_TAU)


def _inproj(x, g1, w_big, qg2, kg2, ba, *, batch, seq, final_layout):
    n = x.shape[0]
    tm = min(INPROJ_TM, seq)
    nlt = seq // tm
    row = lambda w: pl.BlockSpec((tm, w), lambda i: (i, 0))
    full = lambda a: pl.BlockSpec(a.shape, lambda i: (0, 0))
    tail = [(row(512), (n, 512), F32)] * 3 + [(row(256), (n, 256), F32)]
    if final_layout:
        outs = [
            (row(512), (n, 512), BF16),
            (pl.BlockSpec((None, 512, tm), lambda i: (i // nlt, 0, i % nlt)), (batch, 512, seq), F32),
            (pl.BlockSpec((None, None, 512, tm), lambda i: (i // nlt, i % nlt, 0, 0)),
             (batch, nlt, 512, tm), BF16),
            (pl.BlockSpec((tm * A_HEADS, LANES), lambda i: (i, 0)), (n * A_HEADS, LANES), F32),
            (row(512), (n, 512), BF16),
        ] + tail
    else:
        outs = [(row(512), (n, 512), BF16), (row(512), (n, 512), F32), (row(512), (n, 512), F32)] + tail
    return pl.pallas_call(
        functools.partial(_inproj_kernel, final_layout=final_layout),
        grid=(n // tm,),
        in_specs=[row(D_MODEL), full(g1), full(w_big), full(qg2), full(kg2), full(ba)],
        out_specs=[o[0] for o in outs],
        out_shape=[jax.ShapeDtypeStruct(o[1], o[2]) for o in outs],
        compiler_params=_cparams(("parallel",)),
        name="inproj",
    )(x, g1, w_big, qg2, kg2, ba)


def _diff_lambda(lq1_ref, lk1_ref, lq2_ref, lk2_ref, lam_init):
    a = jnp.sum(lq1_ref[...] * lk1_ref[...], axis=-1, keepdims=True)
    b = jnp.sum(lq2_ref[...] * lk2_ref[...], axis=-1, keepdims=True)
    return jnp.exp(a) - jnp.exp(b) + lam_init


def _attn_prompt_kernel(lq1_ref, lk1_ref, lq2_ref, lk2_ref, gain_ref, q_ref, kt_ref, v_ref,
                        o_ref, acc_s, *, seq, lam_init):
    tq, tk, rg = ATT_TQ, ATT_TK, ATT_RG
    nrg = 2 * tq // rg
    lam = _diff_lambda(lq1_ref, lk1_ref, lq2_ref, lk2_ref, lam_init)
    lane = lax.broadcasted_iota(jnp.int32, (tq, LANES), 1)
    lo = lane < A_DH
    ones = jnp.ones((tk, LANES), BF16)
    qq = lax.broadcasted_iota(jnp.int32, (rg, tk), 0)
    kk = lax.broadcasted_iota(jnp.int32, (rg, tk), 1)
    visible = [(kk // CHUNK) <= ((qq + part * rg) // CHUNK) for part in range(tq // rg)]
    unit = jnp.where(lax.broadcasted_iota(jnp.int32, (LANES, tk), 0) == 0, 1.0, 0.0).astype(BF16)

    def key_norms(j, c):
        kb = kt_ref[j].astype(F32)
        sq = kb * kb
        n1 = jnp.max(jnp.sum(sq[:A_DH], axis=0, keepdims=True), axis=1, keepdims=True)
        n2 = jnp.max(jnp.sum(sq[A_DH:], axis=0, keepdims=True), axis=1, keepdims=True)
        return jnp.maximum(c[0], n1), jnp.maximum(c[1], n2)

    zero11 = jnp.zeros((1, 1), F32)
    k1sq, k2sq = lax.fori_loop(0, seq // tk, key_norms, (zero11, zero11))

    def values(j):
        vb = v_ref[pl.ds(pl.multiple_of(j * tk, tk), tk), :]
        return jnp.concatenate([vb, ones], axis=1)

    def shifted_step(j, qx, diagonal):
        ktx = jnp.concatenate([kt_ref[j], unit], axis=0)
        vx = values(j)
        for g in range(nrg):
            rows = slice(g * rg, (g + 1) * rg)
            part = g % (tq // rg)
            nk = (part + 1) * rg if diagonal else tk
            p = jnp.exp2(_dot(qx[rows], ktx[:, :nk]))
            if diagonal:
                p = jnp.where(visible[part][:, :nk], p, 0.0)
            acc_s[rows, :] += _dot(p.astype(BF16), vx[:nk])

    def shifted_pair(jj, carry, qx):
        shifted_step(2 * jj, qx, False)
        shifted_step(2 * jj + 1, qx, False)
        return carry

    def running_max_step(j, ms, qz, diagonal):
        kt = kt_ref[j]
        vx = values(j)
        out = []
        for g in range(nrg):
            rows = slice(g * rg, (g + 1) * rg)
            s = _dot(qz[rows], kt)
            if diagonal:
                s = jnp.where(visible[g % (tq // rg)], s, NEG)
            m_new = jnp.maximum(ms[g], jnp.max(s, axis=-1, keepdims=True))
            alpha = jnp.exp2(ms[g] - m_new)
            p = jnp.exp2(s - m_new).astype(BF16)
            acc_s[rows, :] = alpha * acc_s[rows, :] + _dot(p, vx)
            out.append(m_new)
        return tuple(out)

    def q_block(i, carry):
        qrows = pl.ds(pl.multiple_of(i * tq, tq), tq)
        qi = q_ref[qrows, :]
        zero = jnp.zeros_like(qi)
        q1 = jnp.where(lo, qi, zero)
        q2 = jnp.where(lo, zero, qi)
        qf = qi.astype(F32)
        sq = qf * qf
        shift1 = jnp.sqrt(jnp.sum(jnp.where(lo, sq, 0.0), axis=-1, keepdims=True) * k1sq)
        shift2 = jnp.sqrt(jnp.sum(jnp.where(lo, 0.0, sq), axis=-1, keepdims=True) * k2sq)
        safe = jnp.max(jnp.maximum(shift1, shift2)) <= ATT_SAFE_BOUND
        acc_s[...] = jnp.zeros_like(acc_s)

        @pl.when(safe)
        def _():
            x1 = jnp.where(lane == 0, -shift1, 0.0).astype(BF16)
            x2 = jnp.where(lane == 0, -shift2, 0.0).astype(BF16)
            qx = jnp.concatenate([jnp.concatenate([q1, x1], axis=1), jnp.concatenate([q2, x2], axis=1)], axis=0)
            lax.fori_loop(0, i // 2, lambda jj, c: shifted_pair(jj, c, qx), 0)

            @pl.when(i % 2 == 1)
            def _():
                shifted_step(i - 1, qx, False)

            shifted_step(i, qx, True)

        @pl.when(jnp.logical_not(safe))
        def _():
            qz = jnp.concatenate([q1, q2], axis=0)
            ms = (jnp.full((rg, 1), NEG, F32),) * nrg
            ms = lax.fori_loop(0, i, lambda j, c: running_max_step(j, c, qz, False), ms)
            running_max_step(i, ms, qz, True)

        a1 = acc_s[:tq, :]
        a2 = acc_s[tq:, :]
        o = a1[:, :A_DV] / a1[:, A_DV:] - lam * (a2[:, :A_DV] / a2[:, A_DV:])
        msq = jnp.mean(o * o, axis=-1, keepdims=True)
        o_ref[qrows, :] = (((o * lax.rsqrt(msq + EPS)) * gain_ref[...]) * (1.0 - lam_init)).astype(o_ref.dtype)
        return carry

    lax.fori_loop(0, seq // tq, q_block, 0)


def _attn_prompt(lams, gain_row, q, ktb, v, *, batch, seq, lam_init):
    nkv = seq // ATT_TK
    vec = pl.BlockSpec((1, A_DH), lambda b, h: (0, 0))
    head = pl.BlockSpec((seq, LANES), lambda b, h: (b, h))
    return pl.pallas_call(
        functools.partial(_attn_prompt_kernel, seq=seq, lam_init=lam_init),
        grid=(batch, A_HEADS),
        in_specs=[vec, vec, vec, vec, pl.BlockSpec((1, A_DV), lambda b, h: (0, 0)), head,
                  pl.BlockSpec((None, nkv, LANES, ATT_TK), lambda b, h: (b, 0, h, 0)), head],
        out_specs=head,
        out_shape=jax.ShapeDtypeStruct((batch * seq, A_WIDTH), BF16),
        scratch_shapes=[pltpu.VMEM((2 * ATT_TQ, A_DV + LANES), F32)],
        compiler_params=_cparams(("parallel", "parallel")),
        name="attn_prompt",
    )(*lams, gain_row, q, ktb, v)


def _attn_sample_kernel(lq1_ref, lk1_ref, lq2_ref, lk2_ref, gain_ref, q_ref, kn_ref, vn_ref,
                        kct_ref, vc_ref, o_ref, *, past, lam_init):
    lam = _diff_lambda(lq1_ref, lk1_ref, lq2_ref, lk2_ref, lam_init)
    q = q_ref[...]
    nq = q.shape[0]
    lane = lax.broadcasted_iota(jnp.int32, q.shape, 1)
    zero = jnp.zeros_like(q)
    qz = jnp.concatenate([jnp.where(lane < A_DH, q, zero), jnp.where(lane < A_DH, zero, q)], axis=0)
    vc = vc_ref[pl.ds(pl.program_id(1), past, stride=A_HEADS), :].astype(BF16)
    sc = _dot(qz, kct_ref[...].astype(BF16))
    sn = _nt(qz, kn_ref[...].astype(BF16))
    m = jnp.maximum(jnp.max(sc, axis=-1, keepdims=True), jnp.max(sn, axis=-1, keepdims=True))
    ec = jnp.exp2(sc - m)
    en = jnp.exp2(sn - m)
    l = jnp.sum(ec, axis=-1, keepdims=True) + jnp.sum(en, axis=-1, keepdims=True)
    pv = _dot(ec.astype(BF16), vc) + _dot(en.astype(BF16), vn_ref[...].astype(BF16))
    pv = pv * (1.0 / l)
    o = pv[:nq] - lam * pv[nq:]
    ms = jnp.mean(o * o, axis=-1, keepdims=True)
    o_ref[...] = (((o * lax.rsqrt(ms + EPS)) * gain_ref[...]) * (1.0 - lam_init)).astype(o_ref.dtype)


def _attn_sample(lams, gain_row, q, k, v, kct, vc, *, batch, seq, past, lam_init):
    vec = pl.BlockSpec((1, A_DH), lambda b, h: (0, 0))
    new = pl.BlockSpec((seq, LANES), lambda b, h: (b, h))
    return pl.pallas_call(
        functools.partial(_attn_sample_kernel, past=past, lam_init=lam_init),
        grid=(batch, A_HEADS),
        in_specs=[vec, vec, vec, vec, pl.BlockSpec((1, A_DV), lambda b, h: (0, 0)), new, new, new,
                  pl.BlockSpec((None, LANES, past), lambda b, h: (b, h, 0)),
                  pl.BlockSpec((past * A_HEADS, LANES), lambda b, h: (b, 0))],
        out_specs=new,
        out_shape=jax.ShapeDtypeStruct((batch * seq, A_WIDTH), BF16),
        compiler_params=_cparams(("parallel", "arbitrary")),
        name="attn_sample",
    )(*lams, gain_row, q, k, v, kct, vc)


def _cumsum_rows(x):
    n = x.shape[0]
    row = lax.broadcasted_iota(jnp.int32, x.shape, 0)
    s = 1
    while s < n:
        x = x + jnp.where(row >= s, pltpu.roll(x, s, 0), 0.0)
        s *= 2
    return x


def _gla_kernel(qk_ref, la_ref, v_ref, gr_ref, gain_ref, s0_ref, go_ref, st_ref, st_s, *, chunk, rows_per_step):
    c = chunk
    n_chunks = rows_per_step // c
    unroll = 2 if n_chunks % 2 == 0 else 1

    @pl.when(pl.program_id(1) == 0)
    def _():
        st_s[...] = s0_ref[...]

    lane = lax.broadcasted_iota(jnp.int32, (c, LANES), 1)
    lo = lane < G_DK
    causal = lax.broadcasted_iota(jnp.int32, (c, c), 0) >= lax.broadcasted_iota(jnp.int32, (c, c), 1)
    diag = lax.broadcasted_iota(jnp.int32, (G_DK, LANES), 0) == lax.broadcasted_iota(jnp.int32, (G_DK, LANES), 1)
    scale = G_DK ** -0.5
    pad_lanes = jnp.zeros((c, LANES - c), F32)
    pad_rows = [jnp.zeros((G_DK - c, G_DV), BF16)] if c < G_DK else []

    def one_chunk(ci):
        rows = pl.ds(pl.multiple_of(ci * c, c), c)
        bsum = _cumsum_rows(la_ref[rows, :])
        for h in range(G_HEADS):
            cols = slice(h * LANES, (h + 1) * LANES)
            qk = qk_ref[rows, cols]
            kq = pltpu.roll(qk, G_DK, 1)
            bg = bsum[:, (h // 2) * LANES:(h // 2 + 1) * LANES]
            br = pltpu.roll(bg, G_DK, 1)
            b = jnp.where(lo, bg, br) if h % 2 == 0 else jnp.where(lo, br, bg)
            b_last = b[c - 1:c, :]
            b_mid = b[c // 2 - 1:c // 2, :]
            qt = (qk * jnp.exp(b - b_mid)) * scale
            kt = kq * jnp.exp(b_mid - b)
            qe = (kq * jnp.exp(b)) * scale
            kh = kq * jnp.exp(b_last - b)
            sc = jnp.where(causal, _nt(qt[:, :G_DK].astype(BF16), kt[:, :G_DK].astype(BF16)), 0.0)
            lhs = jnp.where(lo, jnp.concatenate([sc, pad_lanes], axis=1), qe).astype(BF16)
            vb = v_ref[rows, cols].astype(BF16)
            st = st_s[h]
            o = _dot(lhs, jnp.concatenate([vb] + pad_rows + [st.astype(BF16)], axis=0))
            decay = jnp.sum(jnp.where(diag, jnp.exp(b_last), 0.0), axis=-1, keepdims=True)
            st_s[h] = decay * st + _tn(kh[:, :G_DK].astype(BF16), vb)
            ms = jnp.mean(o * o, axis=-1, keepdims=True)
            on = (o * lax.rsqrt(ms + EPS)) * gain_ref[...]
            g = gr_ref[rows, cols]
            go_ref[rows, cols] = (on * (g * jax.nn.sigmoid(g))).astype(go_ref.dtype)

    def chunk_step(i, carry):
        for u in range(unroll):
            one_chunk(i * unroll + u)
        return carry

    lax.fori_loop(0, n_chunks // unroll, chunk_step, 0)
    st_ref[...] = st_s[...]


def _gla(qk, la, v, gr, gain_row, s0, *, batch, seq, chunk):
    lb = min(seq, 1024)
    nl = seq // lb
    wide = pl.BlockSpec((lb, G_WIDTH), lambda b, l: (b * nl + l, 0))
    state = pl.BlockSpec((None, G_HEADS, G_DK, G_DV), lambda b, l: (b, 0, 0, 0))
    return pl.pallas_call(
        functools.partial(_gla_kernel, chunk=chunk, rows_per_step=lb),
        grid=(batch, nl),
        in_specs=[wide, pl.BlockSpec((lb, G_HEADS * G_DK), lambda b, l: (b * nl + l, 0)), wide, wide,
                  pl.BlockSpec((1, G_DV), lambda b, l: (0, 0)), state],
        out_specs=[wide, state],
        out_shape=[jax.ShapeDtypeStruct((batch * seq, G_WIDTH), BF16),
                   jax.ShapeDtypeStruct((batch, G_HEADS, G_DK, G_DV), F32)],
        scratch_shapes=[pltpu.VMEM((G_HEADS, G_DK, G_DV), F32)],
        compiler_params=_cparams(("arbitrary", "arbitrary")),
        name="gla",
    )(qk, la, v, gr, gain_row, s0)


R_E1, R_E2, R_W1, R_W2, R_RANK1, R_RANK2, R_FIELDS = 0, 1, 2, 3, 4, 5, 8


def _route(z, prefix_of):
    lane = lax.broadcasted_iota(jnp.int32, z.shape, 1)
    big = jnp.int32(LANES)

    def first_argmax(vals, vmax):
        return jnp.min(jnp.where(vals == vmax, lane, big), axis=-1, keepdims=True)

    zg = jnp.where(lane < N_GROUPS, z, NEG)
    gmax = jnp.max(zg, axis=-1, keepdims=True)
    g_idx = first_argmax(zg, gmax)
    g_w = 1.0 / jnp.sum(jnp.exp(zg - gmax), axis=-1, keepdims=True)
    first = ROUTE_OFF + EXPERTS_PER_GROUP * g_idx
    ze = jnp.where(lane < first, NEG, jnp.where(lane < first + EXPERTS_PER_GROUP, z, NEG))
    v1 = jnp.max(ze, axis=-1, keepdims=True)
    i1 = first_argmax(ze, v1)
    ze2 = jnp.where(lane == i1, NEG, ze)
    v2 = jnp.max(ze2, axis=-1, keepdims=True)
    i2 = first_argmax(ze2, v2)
    t = jnp.exp(v2 - v1)
    w1 = g_w / (1.0 + t)
    w2 = w1 * t
    hot1 = lane == i1
    hot2 = lane == i2
    one_hot = jnp.where(hot1, 1.0, jnp.where(hot2, 1.0, 0.0))
    before = prefix_of(one_hot)
    rank1 = jnp.sum(jnp.where(hot1, before, 0.0), axis=-1, keepdims=True)
    rank2 = jnp.sum(jnp.where(hot2, before, 0.0), axis=-1, keepdims=True)
    fields = ((R_E1, (i1 - ROUTE_OFF).astype(F32)), (R_E2, (i2 - ROUTE_OFF).astype(F32)), (R_W1, w1), (R_W2, w2),
              (R_RANK1, rank1), (R_RANK2, rank2))
    rec = jnp.zeros(z.shape, F32)
    for pos, val in fields:
        rec = jnp.where(lane == pos, val, rec)
    return rec, one_hot


def _outproj_kernel(ao_ref, go_ref, x_ref, woa_ref, wog_ref, g2_ref, wrh_ref, wrl_ref, br_ref,
                    y1_ref, h2_ref, rec_ref, rect_ref, cnt_ref, cnt_s):
    tm = x_ref.shape[0]

    @pl.when(pl.program_id(0) == 0)
    def _():
        cnt_s[...] = jnp.zeros_like(cnt_s)

    y1 = x_ref[...] + _dot(ao_ref[...], woa_ref[...]) + _dot(go_ref[...], wog_ref[...])
    y1_ref[...] = y1
    ms = jnp.mean(y1 * y1, axis=-1, keepdims=True)
    h2 = (y1 * lax.rsqrt(ms + EPS)) * g2_ref[...]
    hh = h2.astype(BF16)
    hb = hh.astype(F32)
    h2_ref[...] = hb
    hl = (h2 - hb).astype(BF16)
    z = _dot(hh, wrh_ref[...]) + _dot(hl, wrh_ref[...]) + _dot(hh, wrl_ref[...]) + br_ref[...]

    rr = lax.broadcasted_iota(jnp.int32, (tm, tm), 0)
    cc = lax.broadcasted_iota(jnp.int32, (tm, tm), 1)
    earlier = jnp.where(cc < rr, 1.0, 0.0).astype(BF16)

    def prefix_of(one_hot):
        return _dot(earlier, one_hot.astype(BF16)) + cnt_s[...]

    rec, one_hot = _route(z, prefix_of)
    rec_ref[...] = rec
    rect_ref[...] = rec.T[:R_FIELDS, :]
    cnt_s[...] += jnp.sum(one_hot, axis=0, keepdims=True)
    cnt_ref[...] = cnt_s[...]


def _outproj(ao, go, x, woa, wog, g2, wrh, wrl, br):
    n = x.shape[0]
    tm = min(512, n)
    row = lambda w: pl.BlockSpec((tm, w), lambda i: (i, 0))
    full = lambda a: pl.BlockSpec(a.shape, lambda i: (0, 0))
    return pl.pallas_call(
        _outproj_kernel,
        grid=(n // tm,),
        in_specs=[row(A_WIDTH), row(G_WIDTH), row(D_MODEL), full(woa), full(wog), full(g2),
                  full(wrh), full(wrl), full(br)],
        out_specs=[row(D_MODEL), row(D_MODEL), row(LANES),
                   pl.BlockSpec((R_FIELDS, tm), lambda i: (0, i)), pl.BlockSpec((1, LANES), lambda i: (0, 0))],
        out_shape=[jax.ShapeDtypeStruct((n, D_MODEL), F32), jax.ShapeDtypeStruct((n, D_MODEL), F32),
                   jax.ShapeDtypeStruct((n, LANES), F32), jax.ShapeDtypeStruct((R_FIELDS, n), F32),
                   jax.ShapeDtypeStruct((1, LANES), F32)],
        scratch_shapes=[pltpu.VMEM((1, LANES), F32)],
        compiler_params=_cparams(("arbitrary",)),
        name="outproj_router",
    )(ao, go, x, woa, wog, g2, wrh, wrl, br)


def _row(ref, r):
    return ref.at[pl.ds(r, 1), :]


def _dispatch_kernel(pos1_ref, pos2_ref, h_ref, xs_ref, sem):
    tm = h_ref.shape[0]

    def copies(t):
        return [pltpu.make_async_copy(_row(h_ref, t), _row(xs_ref, p[0, t]), sem) for p in (pos1_ref, pos2_ref)]

    def start(t, c):
        for cp in copies(t):
            cp.start()
        return c

    def wait(t, c):
        for cp in copies(t):
            cp.wait()
        return c

    lax.fori_loop(0, tm, start, 0, unroll=MOE_DMA_UNROLL)
    lax.fori_loop(0, tm, wait, 0, unroll=MOE_DMA_UNROLL)


def _dispatch(pos1, pos2, h2, *, tm):
    n = h2.shape[0]
    idx = pl.BlockSpec((None, 1, tm), lambda i: (i, 0, 0), memory_space=pltpu.SMEM)
    return pl.pallas_call(
        _dispatch_kernel,
        grid=(n // tm,),
        in_specs=[idx, idx, pl.BlockSpec((tm, D_MODEL), lambda i: (i, 0))],
        out_specs=pl.BlockSpec(memory_space=pl.ANY),
        out_shape=jax.ShapeDtypeStruct((2 * n, D_MODEL), F32),
        scratch_shapes=[pltpu.SemaphoreType.DMA(())],
        compiler_params=_cparams(("arbitrary",)),
        name="moe_dispatch",
    )(pos1, pos2, h2)


def _experts_kernel(vt_ref, ve_ref, vlo_ref, vhi_ref, vfirst_ref, xs_ref, wgu_ref, wd_ref, ys_ref):
    v = pl.program_id(0)
    t = xs_ref.shape[0]
    rg = min(MOE_ROW_GROUP, t)
    groups = range(t // rg)
    outs = []
    for g in groups:
        gu = _dot(xs_ref[g * rg:(g + 1) * rg, :].astype(BF16), wgu_ref[...])
        gate = gu[:, :D_EXPERT]
        a = ((gate * jax.nn.sigmoid(gate)) * gu[:, D_EXPERT:]).astype(BF16)
        row = lax.broadcasted_iota(jnp.int32, (rg, D_MODEL), 0) + g * rg
        outs.append((_dot(a, wd_ref[...]), jnp.logical_and(row >= vlo_ref[v], row < vhi_ref[v])))

    @pl.when(vfirst_ref[v] == 1)
    def _():
        for g in groups:
            d, mine = outs[g]
            ys_ref[g * rg:(g + 1) * rg, :] = jnp.where(mine, d, 0.0)

    @pl.when(vfirst_ref[v] == 0)
    def _():
        for g in groups:
            d, mine = outs[g]
            ys_ref[g * rg:(g + 1) * rg, :] = jnp.where(mine, d, ys_ref[g * rg:(g + 1) * rg, :])


def _experts(visits, xs, wgu, wd, *, t):
    nv = visits[0].shape[0]
    slot = pl.BlockSpec((t, D_MODEL), lambda v, vt, ve, lo, hi, fi: (vt[v], 0))
    return pl.pallas_call(
        _experts_kernel,
        grid_spec=pltpu.PrefetchScalarGridSpec(
            num_scalar_prefetch=5,
            grid=(nv,),
            in_specs=[slot,
                      pl.BlockSpec((None, D_MODEL, 2 * D_EXPERT), lambda v, vt, ve, lo, hi, fi: (ve[v], 0, 0)),
                      pl.BlockSpec((None, D_EXPERT, D_MODEL), lambda v, vt, ve, lo, hi, fi: (ve[v], 0, 0))],
            out_specs=slot,
        ),
        out_shape=jax.ShapeDtypeStruct(xs.shape, F32),
        compiler_params=_cparams(("arbitrary",)),
        name="moe_experts",
    )(*visits, xs, wgu, wd)


def _combine_kernel(pos1_ref, pos2_ref, y1_ref, rec_ref, ys_ref, out_ref, g1_s, g2_s, sem):
    tm = y1_ref.shape[0]

    def copies(t):
        return [pltpu.make_async_copy(_row(ys_ref, p[0, t]), _row(g, t), sem)
                for p, g in ((pos1_ref, g1_s), (pos2_ref, g2_s))]

    def start(t, c):
        for cp in copies(t):
            cp.start()
        return c

    def wait(t, c):
        for cp in copies(t):
            cp.wait()
        return c

    lax.fori_loop(0, tm, start, 0, unroll=MOE_DMA_UNROLL)
    rec = rec_ref[...]
    w1 = rec[:, R_W1:R_W1 + 1]
    w2 = rec[:, R_W2:R_W2 + 1]
    lax.fori_loop(0, tm, wait, 0, unroll=MOE_DMA_UNROLL)
    out_ref[...] = y1_ref[...] + w1 * g1_s[...] + w2 * g2_s[...]


def _combine(pos1, pos2, y1, rec, ys, *, tm):
    n = y1.shape[0]
    idx = pl.BlockSpec((None, 1, tm), lambda i: (i, 0, 0), memory_space=pltpu.SMEM)
    row = lambda w: pl.BlockSpec((tm, w), lambda i: (i, 0))
    return pl.pallas_call(
        _combine_kernel,
        grid=(n // tm,),
        in_specs=[idx, idx, row(D_MODEL), row(LANES), pl.BlockSpec(memory_space=pl.ANY)],
        out_specs=row(D_MODEL),
        out_shape=jax.ShapeDtypeStruct((n, D_MODEL), F32),
        scratch_shapes=[pltpu.VMEM((tm, D_MODEL), F32), pltpu.VMEM((tm, D_MODEL), F32),
                        pltpu.SemaphoreType.DMA(())],
        compiler_params=_cparams(("arbitrary",)),
        name="moe_combine",
    )(pos1, pos2, y1, rec, ys)


def _expert_visits(counts, n_slots, t):
    n_tiles = n_slots // t
    nv = n_tiles + N_EXPERTS - 1
    end = jnp.cumsum(counts)
    start = end - counts
    first_tile = start // t
    last_tile = jnp.maximum(end - 1, start) // t
    n_vis = jnp.where(counts > 0, last_tile - first_tile + 1, 0)
    vis_end = jnp.cumsum(n_vis)
    total = vis_end[-1]
    v = jnp.arange(nv, dtype=jnp.int32)
    e = jnp.minimum(jnp.sum((vis_end[None, :] <= v[:, None]).astype(jnp.int32), axis=1), N_EXPERTS - 1)
    pick = lambda a: jnp.sum(jnp.where(e[:, None] == jnp.arange(N_EXPERTS)[None, :], a[None, :], 0), axis=1)
    tile = pick(first_tile) + (v - (pick(vis_end) - pick(n_vis)))
    lo = jnp.maximum(pick(start), tile * t) - tile * t
    hi = jnp.minimum(pick(end), (tile + 1) * t) - tile * t
    live = v < total
    last_e = jnp.max(jnp.where(counts > 0, jnp.arange(N_EXPERTS), 0))
    tile = jnp.where(live, tile, n_tiles - 1)
    e = jnp.where(live, e, last_e)
    lo = jnp.where(live, lo, 0)
    hi = jnp.where(live, hi, 0)
    first = jnp.concatenate([jnp.ones((1,), jnp.int32), (tile[1:] != tile[:-1]).astype(jnp.int32)])
    return tuple(a.astype(jnp.int32) for a in (tile, e, lo, hi, first))


def _moe(h2, y1, rec, rect, cnt, wgu, wd):
    n = y1.shape[0]
    tm = min(MOE_TM, n)
    t = min(MOE_SLOT_TILE, 2 * n)
    counts = cnt[0, ROUTE_OFF:ROUTE_OFF + N_EXPERTS].astype(jnp.int32)
    start = jnp.cumsum(counts) - counts
    ids = rect.astype(jnp.int32)
    experts = jnp.arange(N_EXPERTS, dtype=jnp.int32)[:, None]

    def slot(e_row, rank_row):
        return (jnp.sum(jnp.where(ids[e_row][None, :] == experts, start[:, None], 0), axis=0)
                + ids[rank_row]).reshape(n // tm, 1, tm)

    pos1 = slot(R_E1, R_RANK1)
    pos2 = slot(R_E2, R_RANK2)
    xs = _dispatch(pos1, pos2, h2, tm=tm)
    ys = _experts(_expert_visits(counts, 2 * n, t), xs, wgu, wd, t=t)
    return _combine(pos1, pos2, y1, rec, ys, tm=tm)


def _layer_weights(l, norm1_gain, w_in, a_q_gain, a_k_gain, lambda_q1, lambda_k1, lambda_q2, lambda_k2,
                   a_out_gain, w_a2, b_a, g_out_gain, w_out, norm2_gain, w_group, b_group, w_erouter,
                   b_erouter, w_gate, w_up, w_down):
    w = w_in[l]
    w_la = _fold(w[:, _C_GA:_C_GR], w_a2[l])
    qk_cols = []
    for h in range(G_HEADS):
        qk_cols += [w[:, _C_GQ + h * G_DK:_C_GQ + (h + 1) * G_DK], w[:, _C_GK + h * G_DK:_C_GK + (h + 1) * G_DK]]
    w_big = jnp.concatenate([w[:, :_C_GQ]] + qk_cols + [w[:, _C_GV:_C_GA], w[:, _C_GR:], w_la], axis=1).astype(BF16)
    w_r = jnp.concatenate([w_group[l], w_erouter[l],
                           jnp.zeros((D_MODEL, LANES - N_GROUPS - N_EXPERTS), F32)], axis=1)
    w_rh = w_r.astype(BF16)
    w_rl = (w_r - w_rh.astype(F32)).astype(BF16)
    b_r = jnp.concatenate([b_group[l], b_erouter[l], jnp.zeros((LANES - N_GROUPS - N_EXPERTS,), F32)])[None, :]
    return dict(
        g1=norm1_gain[l][None, :], w_big=w_big,
        qg2=jnp.tile(a_q_gain[l], 2)[None, :], kg2=jnp.tile(a_k_gain[l], 2)[None, :],
        ba=b_a[l][None, :],
        lams=(lambda_q1[l][None, :], lambda_k1[l][None, :], lambda_q2[l][None, :], lambda_k2[l][None, :]),
        a_gain=a_out_gain[l][None, :], g_gain=g_out_gain[l][None, :],
        woa=w_out[l][:A_WIDTH].astype(BF16), wog=w_out[l][A_WIDTH:].astype(BF16),
        g2=norm2_gain[l][None, :], w_rh=w_rh, w_rl=w_rl, b_r=b_r,
        wgu=jnp.concatenate([w_gate[l], w_up[l]], axis=-1).astype(BF16), wd=w_down[l].astype(BF16),
    )


def _mix_and_moe(x, p, *, batch, seq, chunk, lam_init, cache=None, state=None):
    prompt = cache is None
    res = _inproj(x, p["g1"], p["w_big"], p["qg2"], p["kg2"], p["ba"], batch=batch, seq=seq, final_layout=prompt)
    if prompt:
        q, kt, ktb, v4, vb, qk, gv, gr, la = res
        ao = _attn_prompt(p["lams"], p["a_gain"], q, ktb, vb, batch=batch, seq=seq, lam_init=lam_init)
        s0 = jnp.zeros((batch, G_HEADS, G_DK, G_DV), F32)
        k_out = jnp.transpose(kt.reshape(batch, A_HEADS, 2, A_DH, seq), (0, 4, 1, 2, 3))
        v_out = v4.reshape(batch, seq, A_HEADS, A_DV)
    else:
        q, k, v, qk, gv, gr, la = res
        kct, vc = cache
        ao = _attn_sample(p["lams"], p["a_gain"], q, k, v, kct, vc, batch=batch, seq=seq,
                          past=kct.shape[-1], lam_init=lam_init)
        s0 = state
        k_out = k.reshape(batch, seq, A_HEADS, 2, A_DH)
        v_out = v.reshape(batch, seq, A_HEADS, A_DV)
    go, st = _gla(qk, la, gv, gr, p["g_gain"], s0, batch=batch, seq=seq, chunk=chunk)
    y1, h2, rec, rect, cnt = _outproj(ao, go, x, p["woa"], p["wog"], p["g2"], p["w_rh"], p["w_rl"], p["b_r"])
    y = _moe(h2, y1, rec, rect, cnt, p["wgu"], p["wd"])
    return y, k_out, v_out, st


def kernel(x_prompt, x_sample, cache_k, cache_v, state_gla, norm1_gain, w_in, a_q_gain, a_k_gain, lambda_q1, lambda_k1, lambda_q2, lambda_k2, a_out_gain, w_a2, b_a, g_out_gain, w_out, norm2_gain, w_group, b_group, w_erouter, b_erouter, w_gate, w_up, w_down):
    weights = (norm1_gain, w_in, a_q_gain, a_k_gain, lambda_q1, lambda_k1, lambda_q2, lambda_k2, a_out_gain,
               w_a2, b_a, g_out_gain, w_out, norm2_gain, w_group, b_group, w_erouter, b_erouter, w_gate,
               w_up, w_down)
    depth = w_in.shape[0]
    pb, pl_, d = x_prompt.shape
    sb, sl, _ = x_sample.shape
    past = cache_k.shape[2]
    y_p = x_prompt.reshape(pb * pl_, d)
    y_s = x_sample.reshape(sb * sl, d)
    outs = [[] for _ in range(6)]
    for l in range(depth):
        lam_init = 0.8 - 0.6 * math.exp(-0.3 * l)
        p = _layer_weights(l, *weights)
        y_p, kp, vp, sp = _mix_and_moe(y_p, p, batch=pb, seq=pl_, chunk=CHUNK, lam_init=lam_init)
        cache = (jnp.transpose(cache_k[l], (0, 2, 3, 4, 1)).reshape(sb, A_WIDTH, past),
                 cache_v[l].reshape(sb * past * A_HEADS, A_DV))
        y_s, kn, vn, sn = _mix_and_moe(y_s, p, batch=sb, seq=sl, chunk=sl, lam_init=lam_init,
                                       cache=cache, state=state_gla[l])
        for o, t in zip(outs, (kp, vp, sp, kn, vn, sn)):
            o.append(t)
    return (y_p.reshape(pb, pl_, d), y_s.reshape(sb, sl, d)) + tuple(jnp.stack(o) for o in outs)
```

```python
import functools
import math

import jax
import jax.numpy as jnp
from jax import lax
from jax.experimental import pallas as pl
from jax.experimental.pallas import tpu as pltpu

F32 = jnp.float32
BF16 = jnp.bfloat16

D_MODEL = 1024
CHUNK = 64
A_HEADS = 4
A_DH = 64
A_DV = 128
A_WIDTH = A_HEADS * A_DV
G_HEADS = 4
G_DK = 64
G_DV = 128
G_WIDTH = G_HEADS * G_DV
G_RANK = 16
G_TAU = 16.0
N_GROUPS = 4
EXPERTS_PER_GROUP = 8
N_EXPERTS = N_GROUPS * EXPERTS_PER_GROUP
D_EXPERT = D_MODEL // 4
EPS = 1e-6

LANES = 128
NEG = -1e30
VMEM_LIMIT = 56 * 1024 * 1024

_C_AQ, _C_AK, _C_AV = 0, 512, 1024
_C_GQ, _C_GK, _C_GV = 1536, 1792, 2048
_C_GA, _C_GR = 2560, 2576
W_BIG = 3328

INPROJ_TM = 512
ATT_TQ = 1024
ATT_TK = 1024
ATT_RG = 256
LOG2E = 1.4426950408889634
ATT_SAFE_BOUND = 43.0 * LOG2E
ROUTE_OFF = N_GROUPS
MOE_TM = 1024
MOE_SLOT_TILE = 512
MOE_ROW_GROUP = 256


def _cparams(sem):
    return pltpu.CompilerParams(dimension_semantics=sem, vmem_limit_bytes=VMEM_LIMIT)


def _nt(a, b):
    return lax.dot_general(a, b, (((1,), (1,)), ((), ())), preferred_element_type=F32)


def _tn(a, b):
    return lax.dot_general(a, b, (((0,), (0,)), ((), ())), preferred_element_type=F32)


def _dot(a, b):
    return jnp.dot(a, b, preferred_element_type=F32)


def _fold_kernel(wga_ref, wa2_ref, out_ref):
    out_ref[...] = jnp.dot(wga_ref[...], wa2_ref[...], preferred_element_type=F32,
                           precision=lax.Precision.HIGHEST)


def _fold(w_ga, w_a2):
    return pl.pallas_call(
        _fold_kernel,
        out_shape=jax.ShapeDtypeStruct((D_MODEL, G_HEADS * G_DK), F32),
        name="fold_gate",
    )(w_ga, w_a2)


def _headnorm(z, gain2):
    outs = []
    lane = lax.broadcasted_iota(jnp.int32, (z.shape[0], LANES), 1)
    lo = lane < A_DH
    for c in range(z.shape[1] // LANES):
        x = z[:, c * LANES:(c + 1) * LANES]
        xx = x * x
        s_lo = jnp.sum(jnp.where(lo, xx, 0.0), axis=-1, keepdims=True)
        s_hi = jnp.sum(jnp.where(lo, 0.0, xx), axis=-1, keepdims=True)
        r = jnp.where(lo, lax.rsqrt(s_lo * (1.0 / A_DH) + EPS), lax.rsqrt(s_hi * (1.0 / A_DH) + EPS))
        outs.append((x * r) * gain2)
    return outs


def _inproj_kernel(x_ref, g1_ref, w_ref, qg_ref, kg_ref, ba_ref, *out_refs, final_layout):
    if final_layout:
        q_ref, kt_ref, ktb_ref, v4_ref, vb_ref, qk_ref, gv_ref, gr_ref, la_ref = out_refs
    else:
        q_ref, k_ref, v_ref, qk_ref, gv_ref, gr_ref, la_ref = out_refs
    x = x_ref[...]
    tm = x.shape[0]
    ms = jnp.mean(x * x, axis=-1, keepdims=True)
    h = ((x * lax.rsqrt(ms + EPS)) * g1_ref[...]).astype(BF16)

    def seg(lo, hi):
        return _dot(h, w_ref[:, lo:hi])

    for c, y in enumerate(_headnorm(seg(0, 512), qg_ref[...])):
        q_ref[:, c * LANES:(c + 1) * LANES] = (y * (A_DH ** -0.5 * LOG2E)).astype(BF16)
    for c, y in enumerate(_headnorm(seg(512, 1024), kg_ref[...])):
        cols = slice(c * LANES, (c + 1) * LANES)
        if final_layout:
            yt = y.T
            kt_ref[cols, :] = yt
            ktb_ref[cols, :] = yt.astype(BF16)
        else:
            k_ref[:, cols] = y
    v = seg(1024, 1536)
    if final_layout:
        vb_ref[...] = v.astype(BF16)
        for c in range(A_HEADS):
            v4_ref[pl.ds(c, tm, stride=A_HEADS), :] = v[:, c * LANES:(c + 1) * LANES]
    else:
        v_ref[...] = v
    qk_ref[...] = seg(1536, 2048)
    gv_ref[...] = seg(2048, 2560)
    gr_ref[...] = seg(2560, 3072)
    zl = seg(3072, 3328) + ba_ref[...]
    la_ref[...] = (jnp.minimum(zl, 0.0) - jnp.log1p(jnp.exp(-jnp.abs(zl)))) * (1.0 / G_TAU)


def _inproj(x, g1, w_big, qg2, kg2, ba, *, batch, seq, final_layout):
    n = x.shape[0]
    tm = min(INPROJ_TM, seq)
    nlt = seq // tm
    row = lambda w: pl.BlockSpec((tm, w), lambda i: (i, 0))
    full = lambda a: pl.BlockSpec(a.shape, lambda i: (0, 0))
    tail = [(row(512), (n, 512), F32)] * 3 + [(row(256), (n, 256), F32)]
    if final_layout:
        outs = [
            (row(512), (n, 512), BF16),
            (pl.BlockSpec((None, 512, tm), lambda i: (i // nlt, 0, i % nlt)), (batch, 512, seq), F32),
            (pl.BlockSpec((None, None, 512, tm), lambda i: (i // nlt, i % nlt, 0, 0)),
             (batch, nlt, 512, tm), BF16),
            (pl.BlockSpec((tm * A_HEADS, LANES), lambda i: (i, 0)), (n * A_HEADS, LANES), F32),
            (row(512), (n, 512), BF16),
        ] + tail
    else:
        outs = [(row(512), (n, 512), BF16), (row(512), (n, 512), F32), (row(512), (n, 512), F32)] + tail
    return pl.pallas_call(
        functools.partial(_inproj_kernel, final_layout=final_layout),
        grid=(n // tm,),
        in_specs=[row(D_MODEL), full(g1), full(w_big), full(qg2), full(kg2), full(ba)],
        out_specs=[o[0] for o in outs],
        out_shape=[jax.ShapeDtypeStruct(o[1], o[2]) for o in outs],
        compiler_params=_cparams(("parallel",)),
        name="inproj",
    )(x, g1, w_big, qg2, kg2, ba)


def _diff_lambda(lq1_ref, lk1_ref, lq2_ref, lk2_ref, lam_init):
    a = jnp.sum(lq1_ref[...] * lk1_ref[...], axis=-1, keepdims=True)
    b = jnp.sum(lq2_ref[...] * lk2_ref[...], axis=-1, keepdims=True)
    return jnp.exp(a) - jnp.exp(b) + lam_init


def _attn_prompt_kernel(lq1_ref, lk1_ref, lq2_ref, lk2_ref, gain_ref, q_ref, kt_ref, v_ref,
                        o_ref, acc_s, *, seq, lam_init):
    tq, tk, rg = ATT_TQ, ATT_TK, ATT_RG
    kb = kt_ref.shape[-1]
    parts = tq // rg
    nrg = 2 * parts
    lam = _diff_lambda(lq1_ref, lk1_ref, lq2_ref, lk2_ref, lam_init)
    lane = lax.broadcasted_iota(jnp.int32, (tq, LANES), 1)
    lo = lane < A_DH
    ones = jnp.ones((tk, LANES), BF16)
    unit = jnp.where(lax.broadcasted_iota(jnp.int32, (LANES, tk), 0) == 0, 1.0, 0.0).astype(BF16)

    def visible(part, nk):
        qq = lax.broadcasted_iota(jnp.int32, (rg, nk), 0) + part * rg
        kk = lax.broadcasted_iota(jnp.int32, (rg, nk), 1)
        return (kk // CHUNK) <= (qq // CHUNK)

    def key_norms(j, c):
        k = kt_ref[j].astype(F32)
        sq = k * k
        n1 = jnp.max(jnp.sum(sq[:A_DH], axis=0, keepdims=True), axis=1, keepdims=True)
        n2 = jnp.max(jnp.sum(sq[A_DH:], axis=0, keepdims=True), axis=1, keepdims=True)
        return jnp.maximum(c[0], n1), jnp.maximum(c[1], n2)

    zero11 = jnp.zeros((1, 1), F32)
    k1sq, k2sq = lax.fori_loop(0, seq // kb, key_norms, (zero11, zero11))

    def keys(j):
        return jnp.concatenate([kt_ref[j * (tk // kb) + b] for b in range(tk // kb)], axis=1)

    def values(j):
        vb = v_ref[pl.ds(pl.multiple_of(j * tk, tk), tk), :]
        return jnp.concatenate([vb, ones], axis=1)

    def shifted_step(j, carry, qx, diagonal):
        ktx = jnp.concatenate([keys(j), unit], axis=0)
        vx = values(j)
        for g in range(nrg):
            rows = slice(g * rg, (g + 1) * rg)
            part = g % parts
            nk = (part + 1) * rg if diagonal else tk
            p = jnp.exp2(_dot(qx[rows], ktx[:, :nk]))
            if diagonal:
                p = jnp.where(visible(part, nk), p, 0.0)
            acc_s[rows, :] += _dot(p.astype(BF16), vx[:nk])
        return carry

    def running_max_step(j, ms, qz, diagonal):
        kt = keys(j)
        vx = values(j)
        out = []
        for g in range(nrg):
            rows = slice(g * rg, (g + 1) * rg)
            s = _dot(qz[rows], kt)
            if diagonal:
                s = jnp.where(visible(g % parts, tk), s, NEG)
            m_new = jnp.maximum(ms[g], jnp.max(s, axis=-1, keepdims=True))
            alpha = jnp.exp2(ms[g] - m_new)
            p = jnp.exp2(s - m_new).astype(BF16)
            acc_s[rows, :] = alpha * acc_s[rows, :] + _dot(p, vx)
            out.append(m_new)
        return tuple(out)

    def q_block(i, carry):
        qrows = pl.ds(pl.multiple_of(i * tq, tq), tq)
        qi = q_ref[qrows, :]
        zero = jnp.zeros_like(qi)
        q1 = jnp.where(lo, qi, zero)
        q2 = jnp.where(lo, zero, qi)
        qf = qi.astype(F32)
        sq = qf * qf
        shift1 = jnp.sqrt(jnp.sum(jnp.where(lo, sq, 0.0), axis=-1, keepdims=True) * k1sq)
        shift2 = jnp.sqrt(jnp.sum(jnp.where(lo, 0.0, sq), axis=-1, keepdims=True) * k2sq)
        safe = jnp.max(jnp.maximum(shift1, shift2)) <= ATT_SAFE_BOUND
        acc_s[...] = jnp.zeros_like(acc_s)

        @pl.when(safe)
        def _():
            x1 = jnp.where(lane == 0, -shift1, 0.0).astype(BF16)
            x2 = jnp.where(lane == 0, -shift2, 0.0).astype(BF16)
            qx = jnp.concatenate([jnp.concatenate([q1, x1], axis=1), jnp.concatenate([q2, x2], axis=1)], axis=0)
            lax.fori_loop(0, i, lambda j, c: shifted_step(j, c, qx, False), 0)
            shifted_step(i, 0, qx, True)

        @pl.when(jnp.logical_not(safe))
        def _():
            qz = jnp.concatenate([q1, q2], axis=0)
            ms = (jnp.full((rg, 1), NEG, F32),) * nrg
            ms = lax.fori_loop(0, i, lambda j, c: running_max_step(j, c, qz, False), ms)
            running_max_step(i, ms, qz, True)

        a1 = acc_s[:tq, :]
        a2 = acc_s[tq:, :]
        o = a1[:, :A_DV] / a1[:, A_DV:] - lam * (a2[:, :A_DV] / a2[:, A_DV:])
        msq = jnp.mean(o * o, axis=-1, keepdims=True)
        o_ref[qrows, :] = (((o * lax.rsqrt(msq + EPS)) * gain_ref[...]) * (1.0 - lam_init)).astype(o_ref.dtype)
        return carry

    lax.fori_loop(0, seq // tq, q_block, 0)


def _attn_prompt(lams, gain_row, q, ktb, v, *, batch, seq, lam_init):
    nkb, kb = ktb.shape[1], ktb.shape[3]
    vec = pl.BlockSpec((1, A_DH), lambda b, h: (0, 0))
    head = pl.BlockSpec((seq, LANES), lambda b, h: (b, h))
    return pl.pallas_call(
        functools.partial(_attn_prompt_kernel, seq=seq, lam_init=lam_init),
        grid=(batch, A_HEADS),
        in_specs=[vec, vec, vec, vec, pl.BlockSpec((1, A_DV), lambda b, h: (0, 0)), head,
                  pl.BlockSpec((None, nkb, LANES, kb), lambda b, h: (b, 0, h, 0)), head],
        out_specs=head,
        out_shape=jax.ShapeDtypeStruct((batch * seq, A_WIDTH), BF16),
        scratch_shapes=[pltpu.VMEM((2 * ATT_TQ, A_DV + LANES), F32)],
        compiler_params=_cparams(("parallel", "parallel")),
        name="attn_prompt",
    )(*lams, gain_row, q, ktb, v)


def _attn_sample_kernel(lq1_ref, lk1_ref, lq2_ref, lk2_ref, gain_ref, q_ref, kn_ref, vn_ref,
                        kct_ref, vc_ref, o_ref, *, past, lam_init):
    lam = _diff_lambda(lq1_ref, lk1_ref, lq2_ref, lk2_ref, lam_init)
    q = q_ref[...]
    nq = q.shape[0]
    lane = lax.broadcasted_iota(jnp.int32, q.shape, 1)
    zero = jnp.zeros_like(q)
    qz = jnp.concatenate([jnp.where(lane < A_DH, q, zero), jnp.where(lane < A_DH, zero, q)], axis=0)
    vc = vc_ref[pl.ds(pl.program_id(1), past, stride=A_HEADS), :].astype(BF16)
    sc = _dot(qz, kct_ref[...].astype(BF16))
    sn = _nt(qz, kn_ref[...].astype(BF16))
    m = jnp.maximum(jnp.max(sc, axis=-1, keepdims=True), jnp.max(sn, axis=-1, keepdims=True))
    ec = jnp.exp2(sc - m)
    en = jnp.exp2(sn - m)
    l = jnp.sum(ec, axis=-1, keepdims=True) + jnp.sum(en, axis=-1, keepdims=True)
    pv = _dot(ec.astype(BF16), vc) + _dot(en.astype(BF16), vn_ref[...].astype(BF16))
    pv = pv * (1.0 / l)
    o = pv[:nq] - lam * pv[nq:]
    ms = jnp.mean(o * o, axis=-1, keepdims=True)
    o_ref[...] = (((o * lax.rsqrt(ms + EPS)) * gain_ref[...]) * (1.0 - lam_init)).astype(o_ref.dtype)


def _attn_sample(lams, gain_row, q, k, v, kct, vc, *, batch, seq, past, lam_init):
    vec = pl.BlockSpec((1, A_DH), lambda b, h: (0, 0))
    new = pl.BlockSpec((seq, LANES), lambda b, h: (b, h))
    return pl.pallas_call(
        functools.partial(_attn_sample_kernel, past=past, lam_init=lam_init),
        grid=(batch, A_HEADS),
        in_specs=[vec, vec, vec, vec, pl.BlockSpec((1, A_DV), lambda b, h: (0, 0)), new, new, new,
                  pl.BlockSpec((None, LANES, past), lambda b, h: (b, h, 0)),
                  pl.BlockSpec((past * A_HEADS, LANES), lambda b, h: (b, 0))],
        out_specs=new,
        out_shape=jax.ShapeDtypeStruct((batch * seq, A_WIDTH), BF16),
        compiler_params=_cparams(("parallel", "arbitrary")),
        name="attn_sample",
    )(*lams, gain_row, q, k, v, kct, vc)


def _cumsum_rows(x):
    n = x.shape[0]
    row = lax.broadcasted_iota(jnp.int32, x.shape, 0)
    s = 1
    while s < n:
        x = x + jnp.where(row >= s, pltpu.roll(x, s, 0), 0.0)
        s *= 2
    return x


def _gla_kernel(qk_ref, la_ref, v_ref, gr_ref, gain_ref, s0_ref, go_ref, st_ref, st_s, *, chunk, rows_per_step):
    c = chunk
    n_chunks = rows_per_step // c
    unroll = 2 if n_chunks % 2 == 0 else 1

    @pl.when(pl.program_id(1) == 0)
    def _():
        st_s[...] = s0_ref[...]

    lane = lax.broadcasted_iota(jnp.int32, (c, LANES), 1)
    lo = lane < G_DK
    causal = lax.broadcasted_iota(jnp.int32, (c, c), 0) >= lax.broadcasted_iota(jnp.int32, (c, c), 1)
    diag = lax.broadcasted_iota(jnp.int32, (G_DK, LANES), 0) == lax.broadcasted_iota(jnp.int32, (G_DK, LANES), 1)
    scale = G_DK ** -0.5
    pad_lanes = jnp.zeros((c, LANES - c), F32)
    pad_rows = [jnp.zeros((G_DK - c, G_DV), BF16)] if c < G_DK else []

    def one_chunk(ci):
        rows = pl.ds(pl.multiple_of(ci * c, c), c)
        bsum = _cumsum_rows(la_ref[rows, :])
        for h in range(G_HEADS):
            cols = slice(h * LANES, (h + 1) * LANES)
            qk = qk_ref[rows, cols]
            kq = pltpu.roll(qk, G_DK, 1)
            bg = bsum[:, (h // 2) * LANES:(h // 2 + 1) * LANES]
            br = pltpu.roll(bg, G_DK, 1)
            b = jnp.where(lo, bg, br) if h % 2 == 0 else jnp.where(lo, br, bg)
            b_last = b[c - 1:c, :]
            b_mid = b[c // 2 - 1:c // 2, :]
            qt = (qk * jnp.exp(b - b_mid)) * scale
            kt = kq * jnp.exp(b_mid - b)
            qe = (kq * jnp.exp(b)) * scale
            kh = kq * jnp.exp(b_last - b)
            sc = jnp.where(causal, _nt(qt[:, :G_DK].astype(BF16), kt[:, :G_DK].astype(BF16)), 0.0)
            lhs = jnp.where(lo, jnp.concatenate([sc, pad_lanes], axis=1), qe).astype(BF16)
            vb = v_ref[rows, cols].astype(BF16)
            st = st_s[h]
            o = _dot(lhs, jnp.concatenate([vb] + pad_rows + [st.astype(BF16)], axis=0))
            decay = jnp.sum(jnp.where(diag, jnp.exp(b_last), 0.0), axis=-1, keepdims=True)
            st_s[h] = decay * st + _tn(kh[:, :G_DK].astype(BF16), vb)
            ms = jnp.mean(o * o, axis=-1, keepdims=True)
            on = (o * lax.rsqrt(ms + EPS)) * gain_ref[...]
            g = gr_ref[rows, cols]
            go_ref[rows, cols] = (on * (g * jax.nn.sigmoid(g))).astype(go_ref.dtype)

    def chunk_step(i, carry):
        for u in range(unroll):
            one_chunk(i * unroll + u)
        return carry

    lax.fori_loop(0, n_chunks // unroll, chunk_step, 0)
    st_ref[...] = st_s[...]


def _gla(qk, la, v, gr, gain_row, s0, *, batch, seq, chunk):
    lb = min(seq, 1024)
    nl = seq // lb
    wide = pl.BlockSpec((lb, G_WIDTH), lambda b, l: (b * nl + l, 0))
    state = pl.BlockSpec((None, G_HEADS, G_DK, G_DV), lambda b, l: (b, 0, 0, 0))
    return pl.pallas_call(
        functools.partial(_gla_kernel, chunk=chunk, rows_per_step=lb),
        grid=(batch, nl),
        in_specs=[wide, pl.BlockSpec((lb, G_HEADS * G_DK), lambda b, l: (b * nl + l, 0)), wide, wide,
                  pl.BlockSpec((1, G_DV), lambda b, l: (0, 0)), state],
        out_specs=[wide, state],
        out_shape=[jax.ShapeDtypeStruct((batch * seq, G_WIDTH), BF16),
                   jax.ShapeDtypeStruct((batch, G_HEADS, G_DK, G_DV), F32)],
        scratch_shapes=[pltpu.VMEM((G_HEADS, G_DK, G_DV), F32)],
        compiler_params=_cparams(("arbitrary", "arbitrary")),
        name="gla",
    )(qk, la, v, gr, gain_row, s0)


R_E1, R_E2, R_W1, R_W2, R_RANK1, R_RANK2, R_FIELDS = 0, 1, 2, 3, 4, 5, 8


def _route(z, prefix_of):
    lane = lax.broadcasted_iota(jnp.int32, z.shape, 1)
    big = jnp.int32(LANES)

    def first_argmax(vals, vmax):
        return jnp.min(jnp.where(vals == vmax, lane, big), axis=-1, keepdims=True)

    zg = jnp.where(lane < N_GROUPS, z, NEG)
    gmax = jnp.max(zg, axis=-1, keepdims=True)
    g_idx = first_argmax(zg, gmax)
    g_w = 1.0 / jnp.sum(jnp.exp(zg - gmax), axis=-1, keepdims=True)
    first = ROUTE_OFF + EXPERTS_PER_GROUP * g_idx
    ze = jnp.where(lane < first, NEG, jnp.where(lane < first + EXPERTS_PER_GROUP, z, NEG))
    v1 = jnp.max(ze, axis=-1, keepdims=True)
    i1 = first_argmax(ze, v1)
    ze2 = jnp.where(lane == i1, NEG, ze)
    v2 = jnp.max(ze2, axis=-1, keepdims=True)
    i2 = first_argmax(ze2, v2)
    t = jnp.exp(v2 - v1)
    w1 = g_w / (1.0 + t)
    w2 = w1 * t
    hot1 = lane == i1
    hot2 = lane == i2
    one_hot = jnp.where(hot1, 1.0, jnp.where(hot2, 1.0, 0.0))
    before = prefix_of(one_hot)
    rank1 = jnp.sum(jnp.where(hot1, before, 0.0), axis=-1, keepdims=True)
    rank2 = jnp.sum(jnp.where(hot2, before, 0.0), axis=-1, keepdims=True)
    fields = ((R_E1, (i1 - ROUTE_OFF).astype(F32)), (R_E2, (i2 - ROUTE_OFF).astype(F32)), (R_W1, w1), (R_W2, w2),
              (R_RANK1, rank1), (R_RANK2, rank2))
    rec = jnp.zeros(z.shape, F32)
    for pos, val in fields:
        rec = jnp.where(lane == pos, val, rec)
    return rec, one_hot


def _outproj_kernel(ao_ref, go_ref, x_ref, woa_ref, wog_ref, g2_ref, wr_ref, br_ref,
                    y1_ref, h2_ref, rec_ref, rect_ref, cnt_ref, cnt_s):
    tm = x_ref.shape[0]

    @pl.when(pl.program_id(0) == 0)
    def _():
        cnt_s[...] = jnp.zeros_like(cnt_s)

    y1 = x_ref[...] + _dot(ao_ref[...], woa_ref[...]) + _dot(go_ref[...], wog_ref[...])
    y1_ref[...] = y1
    ms = jnp.mean(y1 * y1, axis=-1, keepdims=True)
    h2 = (y1 * lax.rsqrt(ms + EPS)) * g2_ref[...]
    hh = h2.astype(BF16)
    hb = hh.astype(F32)
    h2_ref[...] = hb
    hl = (h2 - hb).astype(BF16)
    zz = _dot(hh, wr_ref[...])
    z = zz[:, :LANES] + zz[:, LANES:] + _dot(hl, wr_ref[:, :LANES]) + br_ref[...]

    rr = lax.broadcasted_iota(jnp.int32, (tm, tm), 0)
    cc = lax.broadcasted_iota(jnp.int32, (tm, tm), 1)
    earlier = jnp.where(cc < rr, 1.0, 0.0).astype(BF16)

    def prefix_of(one_hot):
        return _dot(earlier, one_hot.astype(BF16)) + cnt_s[...]

    rec, one_hot = _route(z, prefix_of)
    rec_ref[...] = rec
    rect_ref[...] = rec.T[:R_FIELDS, :]
    cnt_s[...] += jnp.sum(one_hot, axis=0, keepdims=True)
    cnt_ref[...] = cnt_s[...]


def _outproj(ao, go, x, woa, wog, g2, wr, br):
    n = x.shape[0]
    tm = min(512, n)
    row = lambda w: pl.BlockSpec((tm, w), lambda i: (i, 0))
    full = lambda a: pl.BlockSpec(a.shape, lambda i: (0, 0))
    return pl.pallas_call(
        _outproj_kernel,
        grid=(n // tm,),
        in_specs=[row(A_WIDTH), row(G_WIDTH), row(D_MODEL), full(woa), full(wog), full(g2), full(wr), full(br)],
        out_specs=[row(D_MODEL), row(D_MODEL), row(LANES),
                   pl.BlockSpec((R_FIELDS, tm), lambda i: (0, i)), pl.BlockSpec((1, LANES), lambda i: (0, 0))],
        out_shape=[jax.ShapeDtypeStruct((n, D_MODEL), F32), jax.ShapeDtypeStruct((n, D_MODEL), F32),
                   jax.ShapeDtypeStruct((n, LANES), F32), jax.ShapeDtypeStruct((R_FIELDS, n), F32),
                   jax.ShapeDtypeStruct((1, LANES), F32)],
        scratch_shapes=[pltpu.VMEM((1, LANES), F32)],
        compiler_params=_cparams(("arbitrary",)),
        name="outproj_router",
    )(ao, go, x, woa, wog, g2, wr, br)


SUBLANES = 8


def _slot_row(ref, tile_ref, sub_ref, t):
    return ref.at[tile_ref[0, t], pl.ds(sub_ref[0, t], 1), :]


def _dispatch_kernel(t1_ref, s1_ref, t2_ref, s2_ref, h_ref, xs_ref, sem):
    def copies(g, u):
        src = h_ref.at[g, pl.ds(u, 1), :]
        t = g * SUBLANES + u
        return [pltpu.make_async_copy(src, _slot_row(xs_ref, tr, sr, t), sem)
                for tr, sr in ((t1_ref, s1_ref), (t2_ref, s2_ref))]

    def start(g, c):
        for u in range(SUBLANES):
            for k, cp in enumerate(copies(g, u)):
                cp.start(priority=k)
        return c

    def wait(g, c):
        for u in range(SUBLANES):
            for cp in copies(g, u):
                cp.wait()
        return c

    lax.fori_loop(0, h_ref.shape[0], start, 0)
    lax.fori_loop(0, h_ref.shape[0], wait, 0)


def _tiles(a):
    return a.reshape(a.shape[0] // SUBLANES, SUBLANES, a.shape[1])


def _dispatch(slots, h2, *, tm):
    n = h2.shape[0]
    idx = pl.BlockSpec((None, 1, tm), lambda i: (i, 0, 0), memory_space=pltpu.SMEM)
    return pl.pallas_call(
        _dispatch_kernel,
        grid=(n // tm,),
        in_specs=[idx] * 4 + [pl.BlockSpec((tm // SUBLANES, SUBLANES, D_MODEL), lambda i: (i, 0, 0))],
        out_specs=pl.BlockSpec(memory_space=pl.ANY),
        out_shape=jax.ShapeDtypeStruct((2 * n // SUBLANES, SUBLANES, D_MODEL), F32),
        scratch_shapes=[pltpu.SemaphoreType.DMA(())],
        compiler_params=_cparams(("arbitrary",)),
        name="moe_dispatch",
    )(*slots, _tiles(h2)).reshape(2 * n, D_MODEL)


def _experts_kernel(vt_ref, ve_ref, vlo_ref, vhi_ref, vfirst_ref, xs_ref, wg_ref, wu_ref, wd_ref, ys_ref):
    v = pl.program_id(0)
    t = xs_ref.shape[0]
    rg = min(MOE_ROW_GROUP, t)
    groups = range(t // rg)
    wg = wg_ref[...].astype(BF16)
    wu = wu_ref[...].astype(BF16)
    wd = wd_ref[...].astype(BF16)
    outs = []
    for g in groups:
        x = xs_ref[g * rg:(g + 1) * rg, :].astype(BF16)
        gate = _dot(x, wg)
        a = ((gate * jax.nn.sigmoid(gate)) * _dot(x, wu)).astype(BF16)
        row = lax.broadcasted_iota(jnp.int32, (rg, D_MODEL), 0) + g * rg
        outs.append((_dot(a, wd), jnp.logical_and(row >= vlo_ref[v], row < vhi_ref[v])))

    @pl.when(vfirst_ref[v] == 1)
    def _():
        for g in groups:
            d, mine = outs[g]
            ys_ref[g * rg:(g + 1) * rg, :] = jnp.where(mine, d, 0.0)

    @pl.when(vfirst_ref[v] == 0)
    def _():
        for g in groups:
            d, mine = outs[g]
            ys_ref[g * rg:(g + 1) * rg, :] = jnp.where(mine, d, ys_ref[g * rg:(g + 1) * rg, :])


def _experts(visits, xs, w_gate, w_up, w_down, *, t):
    nv = visits[0].shape[0]
    slot = pl.BlockSpec((t, D_MODEL), lambda v, vt, ve, lo, hi, fi: (vt[v], 0))
    w_in = pl.BlockSpec((None, D_MODEL, D_EXPERT), lambda v, vt, ve, lo, hi, fi: (ve[v], 0, 0))
    w_out = pl.BlockSpec((None, D_EXPERT, D_MODEL), lambda v, vt, ve, lo, hi, fi: (ve[v], 0, 0))
    return pl.pallas_call(
        _experts_kernel,
        grid_spec=pltpu.PrefetchScalarGridSpec(
            num_scalar_prefetch=5, grid=(nv,), in_specs=[slot, w_in, w_in, w_out], out_specs=slot),
        out_shape=jax.ShapeDtypeStruct(xs.shape, F32),
        compiler_params=_cparams(("arbitrary",)),
        name="moe_experts",
    )(*visits, xs, w_gate, w_up, w_down)


def _combine_kernel(t1_ref, s1_ref, t2_ref, s2_ref, y1_ref, rec_ref, ys_ref, out_ref, g1_s, g2_s, sem):
    def copies(g, u):
        t = g * SUBLANES + u
        return [pltpu.make_async_copy(_slot_row(ys_ref, tr, sr, t), dst.at[g, pl.ds(u, 1), :], sem)
                for tr, sr, dst in ((t1_ref, s1_ref, g1_s), (t2_ref, s2_ref, g2_s))]

    def start(g, c):
        for u in range(SUBLANES):
            for k, cp in enumerate(copies(g, u)):
                cp.start(priority=k)
        return c

    def wait(g, c):
        for u in range(SUBLANES):
            for cp in copies(g, u):
                cp.wait()
        return c

    lax.fori_loop(0, g1_s.shape[0], start, 0)
    rec = rec_ref[...]
    w1 = rec[:, R_W1:R_W1 + 1]
    w2 = rec[:, R_W2:R_W2 + 1]
    lax.fori_loop(0, g1_s.shape[0], wait, 0)
    tm = y1_ref.shape[0]
    out_ref[...] = (y1_ref[...] + w1 * g1_s[...].reshape(tm, D_MODEL) + w2 * g2_s[...].reshape(tm, D_MODEL))


def _combine(slots, y1, rec, ys, *, tm):
    n = y1.shape[0]
    idx = pl.BlockSpec((None, 1, tm), lambda i: (i, 0, 0), memory_space=pltpu.SMEM)
    row = lambda w: pl.BlockSpec((tm, w), lambda i: (i, 0))
    gathered = pltpu.VMEM((tm // SUBLANES, SUBLANES, D_MODEL), F32)
    return pl.pallas_call(
        _combine_kernel,
        grid=(n // tm,),
        in_specs=[idx] * 4 + [row(D_MODEL), row(LANES), pl.BlockSpec(memory_space=pl.ANY)],
        out_specs=row(D_MODEL),
        out_shape=jax.ShapeDtypeStruct((n, D_MODEL), F32),
        scratch_shapes=[gathered, gathered, pltpu.SemaphoreType.DMA(())],
        compiler_params=_cparams(("arbitrary",)),
        name="moe_combine",
    )(*slots, y1, rec, _tiles(ys))


def _expert_visits(counts, n_slots, t):
    n_tiles = n_slots // t
    nv = n_tiles + N_EXPERTS - 1
    end = jnp.cumsum(counts)
    start = end - counts
    first_tile = start // t
    last_tile = jnp.maximum(end - 1, start) // t
    n_vis = jnp.where(counts > 0, last_tile - first_tile + 1, 0)
    vis_end = jnp.cumsum(n_vis)
    total = vis_end[-1]
    v = jnp.arange(nv, dtype=jnp.int32)
    e = jnp.minimum(jnp.sum((vis_end[None, :] <= v[:, None]).astype(jnp.int32), axis=1), N_EXPERTS - 1)
    pick = lambda a: jnp.sum(jnp.where(e[:, None] == jnp.arange(N_EXPERTS)[None, :], a[None, :], 0), axis=1)
    tile = pick(first_tile) + (v - (pick(vis_end) - pick(n_vis)))
    lo = jnp.maximum(pick(start), tile * t) - tile * t
    hi = jnp.minimum(pick(end), (tile + 1) * t) - tile * t
    live = v < total
    last_e = jnp.max(jnp.where(counts > 0, jnp.arange(N_EXPERTS), 0))
    tile = jnp.where(live, tile, n_tiles - 1)
    e = jnp.where(live, e, last_e)
    lo = jnp.where(live, lo, 0)
    hi = jnp.where(live, hi, 0)
    first = jnp.concatenate([jnp.ones((1,), jnp.int32), (tile[1:] != tile[:-1]).astype(jnp.int32)])
    return tuple(a.astype(jnp.int32) for a in (tile, e, lo, hi, first))


def _moe(h2, y1, rec, rect, cnt, w_gate, w_up, w_down):
    n = y1.shape[0]
    tm = min(MOE_TM, n)
    t = min(MOE_SLOT_TILE, 2 * n)
    counts = cnt[0, ROUTE_OFF:ROUTE_OFF + N_EXPERTS].astype(jnp.int32)
    start = jnp.cumsum(counts) - counts
    ids = rect.astype(jnp.int32)
    experts = jnp.arange(N_EXPERTS, dtype=jnp.int32)[:, None]

    def slot(e_row, rank_row):
        s = jnp.sum(jnp.where(ids[e_row][None, :] == experts, start[:, None], 0), axis=0) + ids[rank_row]
        return [(s // SUBLANES).reshape(n // tm, 1, tm), (s % SUBLANES).reshape(n // tm, 1, tm)]

    slots = slot(R_E1, R_RANK1) + slot(R_E2, R_RANK2)
    xs = _dispatch(slots, h2, tm=tm)
    ys = _experts(_expert_visits(counts, 2 * n, t), xs, w_gate, w_up, w_down, t=t)
    return _combine(slots, y1, rec, ys, tm=tm)


def _layer_weights(l, norm1_gain, w_in, a_q_gain, a_k_gain, lambda_q1, lambda_k1, lambda_q2, lambda_k2,
                   a_out_gain, w_a2, b_a, g_out_gain, w_out, norm2_gain, w_group, b_group, w_erouter,
                   b_erouter, w_gate, w_up, w_down):
    w = w_in[l]
    w_la = _fold(w[:, _C_GA:_C_GR], w_a2[l])
    qk_cols = []
    for h in range(G_HEADS):
        qk_cols += [w[:, _C_GQ + h * G_DK:_C_GQ + (h + 1) * G_DK], w[:, _C_GK + h * G_DK:_C_GK + (h + 1) * G_DK]]
    w_big = jnp.concatenate([w[:, :_C_GQ]] + qk_cols + [w[:, _C_GV:_C_GA], w[:, _C_GR:], w_la], axis=1).astype(BF16)
    w_r = jnp.concatenate([w_group[l], w_erouter[l],
                           jnp.zeros((D_MODEL, LANES - N_GROUPS - N_EXPERTS), F32)], axis=1)
    w_rh = w_r.astype(BF16)
    w_r2 = jnp.concatenate([w_rh, (w_r - w_rh.astype(F32)).astype(BF16)], axis=1)
    b_r = jnp.concatenate([b_group[l], b_erouter[l], jnp.zeros((LANES - N_GROUPS - N_EXPERTS,), F32)])[None, :]
    return dict(
        g1=norm1_gain[l][None, :], w_big=w_big,
        qg2=jnp.tile(a_q_gain[l], 2)[None, :], kg2=jnp.tile(a_k_gain[l], 2)[None, :],
        ba=b_a[l][None, :],
        lams=(lambda_q1[l][None, :], lambda_k1[l][None, :], lambda_q2[l][None, :], lambda_k2[l][None, :]),
        a_gain=a_out_gain[l][None, :], g_gain=g_out_gain[l][None, :],
        woa=w_out[l][:A_WIDTH].astype(BF16), wog=w_out[l][A_WIDTH:].astype(BF16),
        g2=norm2_gain[l][None, :], w_r2=w_r2, b_r=b_r,
        w_gate=w_gate[l], w_up=w_up[l], w_down=w_down[l],
    )


def _mix_and_moe(x, p, *, batch, seq, chunk, lam_init, cache=None, state=None):
    prompt = cache is None
    res = _inproj(x, p["g1"], p["w_big"], p["qg2"], p["kg2"], p["ba"], batch=batch, seq=seq, final_layout=prompt)
    if prompt:
        q, kt, ktb, v4, vb, qk, gv, gr, la = res
        ao = _attn_prompt(p["lams"], p["a_gain"], q, ktb, vb, batch=batch, seq=seq, lam_init=lam_init)
        s0 = jnp.zeros((batch, G_HEADS, G_DK, G_DV), F32)
        k_out = jnp.transpose(kt.reshape(batch, A_HEADS, 2, A_DH, seq), (0, 4, 1, 2, 3))
        v_out = v4.reshape(batch, seq, A_HEADS, A_DV)
    else:
        q, k, v, qk, gv, gr, la = res
        kct, vc = cache
        ao = _attn_sample(p["lams"], p["a_gain"], q, k, v, kct, vc, batch=batch, seq=seq,
                          past=kct.shape[-1], lam_init=lam_init)
        s0 = state
        k_out = k.reshape(batch, seq, A_HEADS, 2, A_DH)
        v_out = v.reshape(batch, seq, A_HEADS, A_DV)
    go, st = _gla(qk, la, gv, gr, p["g_gain"], s0, batch=batch, seq=seq, chunk=chunk)
    y1, h2, rec, rect, cnt = _outproj(ao, go, x, p["woa"], p["wog"], p["g2"], p["w_r2"], p["b_r"])
    y = _moe(h2, y1, rec, rect, cnt, p["w_gate"], p["w_up"], p["w_down"])
    return y, k_out, v_out, st


def kernel(x_prompt, x_sample, cache_k, cache_v, state_gla, norm1_gain, w_in, a_q_gain, a_k_gain, lambda_q1, lambda_k1, lambda_q2, lambda_k2, a_out_gain, w_a2, b_a, g_out_gain, w_out, norm2_gain, w_group, b_group, w_erouter, b_erouter, w_gate, w_up, w_down):
    weights = (norm1_gain, w_in, a_q_gain, a_k_gain, lambda_q1, lambda_k1, lambda_q2, lambda_k2, a_out_gain,
               w_a2, b_a, g_out_gain, w_out, norm2_gain, w_group, b_group, w_erouter, b_erouter, w_gate,
               w_up, w_down)
    depth = w_in.shape[0]
    pb, pl_, d = x_prompt.shape
    sb, sl, _ = x_sample.shape
    past = cache_k.shape[2]
    y_p = x_prompt.reshape(pb * pl_, d)
    y_s = x_sample.reshape(sb * sl, d)
    outs = [[] for _ in range(6)]
    for l in range(depth):
        lam_init = 0.8 - 0.6 * math.exp(-0.3 * l)
        p = _layer_weights(l, *weights)
        y_p, kp, vp, sp = _mix_and_moe(y_p, p, batch=pb, seq=pl_, chunk=CHUNK, lam_init=lam_init)
        cache = (jnp.transpose(cache_k[l], (0, 2, 3, 4, 1)).reshape(sb, A_WIDTH, past),
                 cache_v[l].reshape(sb * past * A_HEADS, A_DV))
        y_s, kn, vn, sn = _mix_and_moe(y_s, p, batch=sb, seq=sl, chunk=sl, lam_init=lam_init,
                                       cache=cache, state=state_gla[l])
        for o, t in zip(outs, (kp, vp, sp, kn, vn, sn)):
            o.append(t)
    return (y_p.reshape(pb, pl_, d), y_s.reshape(sb, sl, d)) + tuple(jnp.stack(o) for o in outs)
```

```python
import functools
import math

import jax
import jax.numpy as jnp
from jax import lax
from jax.experimental import pallas as pl
from jax.experimental.pallas import tpu as pltpu

F32 = jnp.float32
BF16 = jnp.bfloat16

D_MODEL = 1024
CHUNK = 64
A_HEADS = 4
A_DH = 64
A_DV = 128
A_WIDTH = A_HEADS * A_DV
G_HEADS = 4
G_DK = 64
G_DV = 128
G_WIDTH = G_HEADS * G_DV
G_RANK = 16
G_TAU = 16.0
N_GROUPS = 4
EXPERTS_PER_GROUP = 8
N_EXPERTS = N_GROUPS * EXPERTS_PER_GROUP
D_EXPERT = D_MODEL // 4
EPS = 1e-6

LANES = 128
NEG = -1e30
VMEM_LIMIT = 56 * 1024 * 1024

_C_AQ, _C_AK, _C_AV = 0, 512, 1024
_C_GQ, _C_GK, _C_GV = 1536, 1792, 2048
_C_GA, _C_GR = 2560, 2576
W_BIG = 3328

INPROJ_TM = 512
ATT_TQ = 1024
ATT_TK = 1024
ATT_RG = 256
LOG2E = 1.4426950408889634
ATT_SAFE_BOUND = 43.0 * LOG2E
GLA_UNROLL = 4
ROUTE_OFF = N_GROUPS
MOE_TM = 1024
MOE_DMA_UNROLL = 8
MOE_SLOT_TILE = 512
MOE_ROW_GROUP = 256


def _cparams(sem):
    return pltpu.CompilerParams(dimension_semantics=sem, vmem_limit_bytes=VMEM_LIMIT)


def _nt(a, b):
    return lax.dot_general(a, b, (((1,), (1,)), ((), ())), preferred_element_type=F32)


def _tn(a, b):
    return lax.dot_general(a, b, (((0,), (0,)), ((), ())), preferred_element_type=F32)


def _dot(a, b):
    return jnp.dot(a, b, preferred_element_type=F32)


def _fold_kernel(wga_ref, wa2_ref, out_ref):
    out_ref[...] = jnp.dot(wga_ref[...], wa2_ref[...], preferred_element_type=F32,
                           precision=lax.Precision.HIGHEST)


def _fold(w_ga, w_a2):
    return pl.pallas_call(
        _fold_kernel,
        out_shape=jax.ShapeDtypeStruct((D_MODEL, G_HEADS * G_DK), F32),
        name="fold_gate",
    )(w_ga, w_a2)


def _headnorm(z, gain2):
    outs = []
    lane = lax.broadcasted_iota(jnp.int32, (z.shape[0], LANES), 1)
    lo = lane < A_DH
    for c in range(z.shape[1] // LANES):
        x = z[:, c * LANES:(c + 1) * LANES]
        xx = x * x
        s_lo = jnp.sum(jnp.where(lo, xx, 0.0), axis=-1, keepdims=True)
        s_hi = jnp.sum(jnp.where(lo, 0.0, xx), axis=-1, keepdims=True)
        r = jnp.where(lo, lax.rsqrt(s_lo * (1.0 / A_DH) + EPS), lax.rsqrt(s_hi * (1.0 / A_DH) + EPS))
        outs.append((x * r) * gain2)
    return outs


def _inproj_kernel(x_ref, g1_ref, w_ref, qg_ref, kg_ref, ba_ref, *out_refs, final_layout):
    if final_layout:
        q_ref, kt_ref, ktb_ref, v4_ref, vb_ref, qk_ref, gv_ref, gr_ref, la_ref = out_refs
    else:
        q_ref, k_ref, v_ref, qk_ref, gv_ref, gr_ref, la_ref = out_refs
    x = x_ref[...]
    tm = x.shape[0]
    ms = jnp.mean(x * x, axis=-1, keepdims=True)
    h = ((x * lax.rsqrt(ms + EPS)) * g1_ref[...]).astype(BF16)

    def seg(lo, hi):
        return _dot(h, w_ref[:, lo:hi])

    for c, y in enumerate(_headnorm(seg(0, 512), qg_ref[...])):
        q_ref[:, c * LANES:(c + 1) * LANES] = (y * (A_DH ** -0.5 * LOG2E)).astype(BF16)
    for c, y in enumerate(_headnorm(seg(512, 1024), kg_ref[...])):
        cols = slice(c * LANES, (c + 1) * LANES)
        if final_layout:
            yt = y.T
            kt_ref[cols, :] = yt
            ktb_ref[cols, :] = yt.astype(BF16)
        else:
            k_ref[:, cols] = y
    v = seg(1024, 1536)
    if final_layout:
        vb_ref[...] = v.astype(BF16)
        for c in range(A_HEADS):
            v4_ref[pl.ds(c, tm, stride=A_HEADS), :] = v[:, c * LANES:(c + 1) * LANES]
    else:
        v_ref[...] = v
    qk_ref[...] = seg(1536, 2048)
    gv_ref[...] = seg(2048, 2560)
    gr_ref[...] = seg(2560, 3072)
    zl = seg(3072, 3328) + ba_ref[...]
    la_ref[...] = (jnp.minimum(zl, 0.0) - jnp.log1p(jnp.exp(-jnp.abs(zl)))) * (1.0 / G_TAU)


def _inproj(x, g1, w_big, qg2, kg2, ba, *, batch, seq, final_layout):
    n = x.shape[0]
    tm = min(INPROJ_TM, seq)
    nlt = seq // tm
    row = lambda w: pl.BlockSpec((tm, w), lambda i: (i, 0))
    full = lambda a: pl.BlockSpec(a.shape, lambda i: (0, 0))
    tail = [(row(512), (n, 512), F32)] * 3 + [(row(256), (n, 256), F32)]
    if final_layout:
        outs = [
            (row(512), (n, 512), BF16),
            (pl.BlockSpec((None, 512, tm), lambda i: (i // nlt, 0, i % nlt)), (batch, 512, seq), F32),
            (pl.BlockSpec((None, None, 512, tm), lambda i: (i // nlt, i % nlt, 0, 0)),
             (batch, nlt, 512, tm), BF16),
            (pl.BlockSpec((tm * A_HEADS, LANES), lambda i: (i, 0)), (n * A_HEADS, LANES), F32),
            (row(512), (n, 512), BF16),
        ] + tail
    else:
        outs = [(row(512), (n, 512), BF16), (row(512), (n, 512), F32), (row(512), (n, 512), F32)] + tail
    return pl.pallas_call(
        functools.partial(_inproj_kernel, final_layout=final_layout),
        grid=(n // tm,),
        in_specs=[row(D_MODEL), full(g1), full(w_big), full(qg2), full(kg2), full(ba)],
        out_specs=[o[0] for o in outs],
        out_shape=[jax.ShapeDtypeStruct(o[1], o[2]) for o in outs],
        compiler_params=_cparams(("parallel",)),
        name="inproj",
    )(x, g1, w_big, qg2, kg2, ba)


def _diff_lambda(lq1_ref, lk1_ref, lq2_ref, lk2_ref, lam_init):
    a = jnp.sum(lq1_ref[...] * lk1_ref[...], axis=-1, keepdims=True)
    b = jnp.sum(lq2_ref[...] * lk2_ref[...], axis=-1, keepdims=True)
    return jnp.exp(a) - jnp.exp(b) + lam_init


def _attn_prompt_kernel(lq1_ref, lk1_ref, lq2_ref, lk2_ref, gain_ref, q_ref, kt_ref, v_ref,
                        o_ref, acc_s, *, seq, lam_init):
    tq, tk, rg = ATT_TQ, ATT_TK, ATT_RG
    kb = kt_ref.shape[-1]
    parts = tq // rg
    nrg = 2 * parts
    lam = _diff_lambda(lq1_ref, lk1_ref, lq2_ref, lk2_ref, lam_init)
    lane = lax.broadcasted_iota(jnp.int32, (tq, LANES), 1)
    lo = lane < A_DH
    ones = jnp.ones((tk, LANES), BF16)
    unit = jnp.where(lax.broadcasted_iota(jnp.int32, (LANES, tk), 0) == 0, 1.0, 0.0).astype(BF16)

    def visible(part, nk):
        qq = lax.broadcasted_iota(jnp.int32, (rg, nk), 0) + part * rg
        kk = lax.broadcasted_iota(jnp.int32, (rg, nk), 1)
        return (kk // CHUNK) <= (qq // CHUNK)

    def key_norms(j, c):
        k = kt_ref[j].astype(F32)
        sq = k * k
        n1 = jnp.max(jnp.sum(sq[:A_DH], axis=0, keepdims=True), axis=1, keepdims=True)
        n2 = jnp.max(jnp.sum(sq[A_DH:], axis=0, keepdims=True), axis=1, keepdims=True)
        return jnp.maximum(c[0], n1), jnp.maximum(c[1], n2)

    zero11 = jnp.zeros((1, 1), F32)
    k1sq, k2sq = lax.fori_loop(0, seq // kb, key_norms, (zero11, zero11))

    def keys(j):
        return jnp.concatenate([kt_ref[j * (tk // kb) + b] for b in range(tk // kb)], axis=1)

    def values(j):
        vb = v_ref[pl.ds(pl.multiple_of(j * tk, tk), tk), :]
        return jnp.concatenate([vb, ones], axis=1)

    def shifted_step(j, carry, qx, diagonal):
        ktx = jnp.concatenate([keys(j), unit], axis=0)
        vx = values(j)
        for g in range(nrg):
            rows = slice(g * rg, (g + 1) * rg)
            part = g % parts
            nk = (part + 1) * rg if diagonal else tk
            p = jnp.exp2(_dot(qx[rows], ktx[:, :nk]))
            if diagonal:
                p = jnp.where(visible(part, nk), p, 0.0)
            acc_s[rows, :] += _dot(p.astype(BF16), vx[:nk])
        return carry

    def running_max_step(j, ms, qz, diagonal):
        kt = keys(j)
        vx = values(j)
        out = []
        for g in range(nrg):
            rows = slice(g * rg, (g + 1) * rg)
            s = _dot(qz[rows], kt)
            if diagonal:
                s = jnp.where(visible(g % parts, tk), s, NEG)
            m_new = jnp.maximum(ms[g], jnp.max(s, axis=-1, keepdims=True))
            alpha = jnp.exp2(ms[g] - m_new)
            p = jnp.exp2(s - m_new).astype(BF16)
            acc_s[rows, :] = alpha * acc_s[rows, :] + _dot(p, vx)
            out.append(m_new)
        return tuple(out)

    def q_block(i, carry):
        qrows = pl.ds(pl.multiple_of(i * tq, tq), tq)
        qi = q_ref[qrows, :]
        zero = jnp.zeros_like(qi)
        q1 = jnp.where(lo, qi, zero)
        q2 = jnp.where(lo, zero, qi)
        qf = qi.astype(F32)
        sq = qf * qf
        shift1 = jnp.sqrt(jnp.sum(jnp.where(lo, sq, 0.0), axis=-1, keepdims=True) * k1sq)
        shift2 = jnp.sqrt(jnp.sum(jnp.where(lo, 0.0, sq), axis=-1, keepdims=True) * k2sq)
        safe = jnp.max(jnp.maximum(shift1, shift2)) <= ATT_SAFE_BOUND
        acc_s[...] = jnp.zeros_like(acc_s)

        @pl.when(safe)
        def _():
            x1 = jnp.where(lane == 0, -shift1, 0.0).astype(BF16)
            x2 = jnp.where(lane == 0, -shift2, 0.0).astype(BF16)
            qx = jnp.concatenate([jnp.concatenate([q1, x1], axis=1), jnp.concatenate([q2, x2], axis=1)], axis=0)
            lax.fori_loop(0, i, lambda j, c: shifted_step(j, c, qx, False), 0)
            shifted_step(i, 0, qx, True)

        @pl.when(jnp.logical_not(safe))
        def _():
            qz = jnp.concatenate([q1, q2], axis=0)
            ms = (jnp.full((rg, 1), NEG, F32),) * nrg
            ms = lax.fori_loop(0, i, lambda j, c: running_max_step(j, c, qz, False), ms)
            running_max_step(i, ms, qz, True)

        a1 = acc_s[:tq, :]
        a2 = acc_s[tq:, :]
        o = a1[:, :A_DV] / a1[:, A_DV:] - lam * (a2[:, :A_DV] / a2[:, A_DV:])
        msq = jnp.mean(o * o, axis=-1, keepdims=True)
        o_ref[qrows, :] = (((o * lax.rsqrt(msq + EPS)) * gain_ref[...]) * (1.0 - lam_init)).astype(o_ref.dtype)
        return carry

    lax.fori_loop(0, seq // tq, q_block, 0)


def _attn_prompt(lams, gain_row, q, ktb, v, *, batch, seq, lam_init):
    nkb, kb = ktb.shape[1], ktb.shape[3]
    assert seq % ATT_TQ == 0 and ATT_TQ == ATT_TK and ATT_TK % kb == 0, (seq, kb)
    vec = pl.BlockSpec((1, A_DH), lambda b, h: (0, 0))
    head = pl.BlockSpec((seq, LANES), lambda b, h: (b, h))
    return pl.pallas_call(
        functools.partial(_attn_prompt_kernel, seq=seq, lam_init=lam_init),
        grid=(batch, A_HEADS),
        in_specs=[vec, vec, vec, vec, pl.BlockSpec((1, A_DV), lambda b, h: (0, 0)), head,
                  pl.BlockSpec((None, nkb, LANES, kb), lambda b, h: (b, 0, h, 0)), head],
        out_specs=head,
        out_shape=jax.ShapeDtypeStruct((batch * seq, A_WIDTH), BF16),
        scratch_shapes=[pltpu.VMEM((2 * ATT_TQ, A_DV + LANES), F32)],
        compiler_params=_cparams(("parallel", "parallel")),
        name="attn_prompt",
    )(*lams, gain_row, q, ktb, v)


def _attn_sample_kernel(lq1_ref, lk1_ref, lq2_ref, lk2_ref, gain_ref, q_ref, kn_ref, vn_ref,
                        kct_ref, vc_ref, o_ref, *, past, lam_init):
    lam = _diff_lambda(lq1_ref, lk1_ref, lq2_ref, lk2_ref, lam_init)
    nq = q_ref.shape[0]
    lane = lax.broadcasted_iota(jnp.int32, (nq, LANES), 1)
    for h in range(A_HEADS):
        cols = slice(h * LANES, (h + 1) * LANES)
        q = q_ref[:, cols]
        zero = jnp.zeros_like(q)
        qz = jnp.concatenate([jnp.where(lane < A_DH, q, zero), jnp.where(lane < A_DH, zero, q)], axis=0)
        vc = vc_ref[pl.ds(h, past, stride=A_HEADS), :].astype(BF16)
        sc = _dot(qz, kct_ref[cols, :].astype(BF16))
        sn = _nt(qz, kn_ref[:, cols].astype(BF16))
        m = jnp.maximum(jnp.max(sc, axis=-1, keepdims=True), jnp.max(sn, axis=-1, keepdims=True))
        ec = jnp.exp2(sc - m)
        en = jnp.exp2(sn - m)
        l = jnp.sum(ec, axis=-1, keepdims=True) + jnp.sum(en, axis=-1, keepdims=True)
        pv = _dot(ec.astype(BF16), vc) + _dot(en.astype(BF16), vn_ref[:, cols].astype(BF16))
        pv = pv * (1.0 / l)
        o = pv[:nq] - lam * pv[nq:]
        ms = jnp.mean(o * o, axis=-1, keepdims=True)
        o_ref[:, cols] = (((o * lax.rsqrt(ms + EPS)) * gain_ref[...]) * (1.0 - lam_init)).astype(o_ref.dtype)


def _attn_sample(lams, gain_row, q, k, v, kct, vc, *, batch, seq, past, lam_init):
    vec = pl.BlockSpec((1, A_DH), lambda b: (0, 0))
    new = pl.BlockSpec((seq, A_WIDTH), lambda b: (b, 0))
    return pl.pallas_call(
        functools.partial(_attn_sample_kernel, past=past, lam_init=lam_init),
        grid=(batch,),
        in_specs=[vec, vec, vec, vec, pl.BlockSpec((1, A_DV), lambda b: (0, 0)), new, new, new,
                  pl.BlockSpec((None, A_WIDTH, past), lambda b: (b, 0, 0)),
                  pl.BlockSpec((past * A_HEADS, LANES), lambda b: (b, 0))],
        out_specs=new,
        out_shape=jax.ShapeDtypeStruct((batch * seq, A_WIDTH), BF16),
        compiler_params=_cparams(("parallel",)),
        name="attn_sample",
    )(*lams, gain_row, q, k, v, kct, vc)


def _cumsum_rows(x):
    n = x.shape[0]
    row = lax.broadcasted_iota(jnp.int32, x.shape, 0)
    s = 1
    while s < n:
        x = x + jnp.where(row >= s, pltpu.roll(x, s, 0), 0.0)
        s *= 2
    return x


def _gla_kernel(qk_ref, la_ref, v_ref, gr_ref, gain_ref, s0_ref, go_ref, st_ref, st_s, *, chunk, rows_per_step):
    c = chunk
    n_chunks = rows_per_step // c
    unroll = GLA_UNROLL if n_chunks % GLA_UNROLL == 0 else 1

    @pl.when(pl.program_id(1) == 0)
    def _():
        st_s[...] = s0_ref[...]

    lane = lax.broadcasted_iota(jnp.int32, (c, LANES), 1)
    lo = lane < G_DK
    causal = lax.broadcasted_iota(jnp.int32, (c, c), 0) >= lax.broadcasted_iota(jnp.int32, (c, c), 1)
    diag = lax.broadcasted_iota(jnp.int32, (G_DK, LANES), 0) == lax.broadcasted_iota(jnp.int32, (G_DK, LANES), 1)
    scale = G_DK ** -0.5
    pad_lanes = jnp.zeros((c, LANES - c), F32)
    pad_rows = [jnp.zeros((G_DK - c, G_DV), BF16)] if c < G_DK else []

    def one_chunk(ci):
        rows = pl.ds(pl.multiple_of(ci * c, c), c)
        bsum = _cumsum_rows(la_ref[rows, :])
        for h in range(G_HEADS):
            cols = slice(h * LANES, (h + 1) * LANES)
            qk = qk_ref[rows, cols]
            kq = pltpu.roll(qk, G_DK, 1)
            bg = bsum[:, (h // 2) * LANES:(h // 2 + 1) * LANES]
            br = pltpu.roll(bg, G_DK, 1)
            b = jnp.where(lo, bg, br) if h % 2 == 0 else jnp.where(lo, br, bg)
            b_last = b[c - 1:c, :]
            b_mid = b[c // 2 - 1:c // 2, :]
            qt = (qk * jnp.exp(b - b_mid)) * scale
            kt = kq * jnp.exp(b_mid - b)
            qe = (kq * jnp.exp(b)) * scale
            kh = kq * jnp.exp(b_last - b)
            sc = jnp.where(causal, _nt(qt[:, :G_DK].astype(BF16), kt[:, :G_DK].astype(BF16)), 0.0)
            lhs = jnp.where(lo, jnp.concatenate([sc, pad_lanes], axis=1), qe).astype(BF16)
            vb = v_ref[rows, cols].astype(BF16)
            st = st_s[h]
            o = _dot(lhs, jnp.concatenate([vb] + pad_rows + [st.astype(BF16)], axis=0))
            decay = jnp.sum(jnp.where(diag, jnp.exp(b_last), 0.0), axis=-1, keepdims=True)
            st_s[h] = decay * st + _tn(kh[:, :G_DK].astype(BF16), vb)
            ms = jnp.mean(o * o, axis=-1, keepdims=True)
            on = (o * lax.rsqrt(ms + EPS)) * gain_ref[...]
            g = gr_ref[rows, cols]
            go_ref[rows, cols] = (on * (g * jax.nn.sigmoid(g))).astype(go_ref.dtype)

    def chunk_step(i, carry):
        for u in range(unroll):
            one_chunk(i * unroll + u)
        return carry

    lax.fori_loop(0, n_chunks // unroll, chunk_step, 0)
    st_ref[...] = st_s[...]


def _gla(qk, la, v, gr, gain_row, s0, *, batch, seq, chunk):
    lb = min(seq, 1024)
    nl = seq // lb
    wide = pl.BlockSpec((lb, G_WIDTH), lambda b, l: (b * nl + l, 0))
    state = pl.BlockSpec((None, G_HEADS, G_DK, G_DV), lambda b, l: (b, 0, 0, 0))
    return pl.pallas_call(
        functools.partial(_gla_kernel, chunk=chunk, rows_per_step=lb),
        grid=(batch, nl),
        in_specs=[wide, pl.BlockSpec((lb, G_HEADS * G_DK), lambda b, l: (b * nl + l, 0)), wide, wide,
                  pl.BlockSpec((1, G_DV), lambda b, l: (0, 0)), state],
        out_specs=[wide, state],
        out_shape=[jax.ShapeDtypeStruct((batch * seq, G_WIDTH), BF16),
                   jax.ShapeDtypeStruct((batch, G_HEADS, G_DK, G_DV), F32)],
        scratch_shapes=[pltpu.VMEM((G_HEADS, G_DK, G_DV), F32)],
        compiler_params=_cparams(("arbitrary", "arbitrary")),
        name="gla",
    )(qk, la, v, gr, gain_row, s0)


R_E1, R_E2, R_W1, R_W2, R_RANK1, R_RANK2, R_FIELDS = 0, 1, 2, 3, 4, 5, 8


def _route(z, prefix_of):
    lane = lax.broadcasted_iota(jnp.int32, z.shape, 1)
    big = jnp.int32(LANES)

    def first_argmax(vals, vmax):
        return jnp.min(jnp.where(vals == vmax, lane, big), axis=-1, keepdims=True)

    zg = jnp.where(lane < N_GROUPS, z, NEG)
    gmax = jnp.max(zg, axis=-1, keepdims=True)
    g_idx = first_argmax(zg, gmax)
    g_w = 1.0 / jnp.sum(jnp.exp(zg - gmax), axis=-1, keepdims=True)
    first = ROUTE_OFF + EXPERTS_PER_GROUP * g_idx
    ze = jnp.where(lane < first, NEG, jnp.where(lane < first + EXPERTS_PER_GROUP, z, NEG))
    v1 = jnp.max(ze, axis=-1, keepdims=True)
    i1 = first_argmax(ze, v1)
    ze2 = jnp.where(lane == i1, NEG, ze)
    v2 = jnp.max(ze2, axis=-1, keepdims=True)
    i2 = first_argmax(ze2, v2)
    t = jnp.exp(v2 - v1)
    w1 = g_w / (1.0 + t)
    w2 = w1 * t
    hot1 = lane == i1
    hot2 = lane == i2
    one_hot = jnp.where(hot1, 1.0, jnp.where(hot2, 1.0, 0.0))
    before = prefix_of(one_hot)
    rank1 = jnp.sum(jnp.where(hot1, before, 0.0), axis=-1, keepdims=True)
    rank2 = jnp.sum(jnp.where(hot2, before, 0.0), axis=-1, keepdims=True)
    fields = ((R_E1, (i1 - ROUTE_OFF).astype(F32)), (R_E2, (i2 - ROUTE_OFF).astype(F32)), (R_W1, w1), (R_W2, w2),
              (R_RANK1, rank1), (R_RANK2, rank2))
    rec = jnp.zeros(z.shape, F32)
    for pos, val in fields:
        rec = jnp.where(lane == pos, val, rec)
    return rec, one_hot


def _outproj_kernel(ao_ref, go_ref, x_ref, woa_ref, wog_ref, g2_ref, wr_ref, br_ref,
                    y1_ref, h2_ref, rec_ref, rect_ref, cnt_ref, cnt_s):
    tm = x_ref.shape[0]

    @pl.when(pl.program_id(0) == 0)
    def _():
        cnt_s[...] = jnp.zeros_like(cnt_s)

    y1 = x_ref[...] + _dot(ao_ref[...], woa_ref[...]) + _dot(go_ref[...], wog_ref[...])
    y1_ref[...] = y1
    ms = jnp.mean(y1 * y1, axis=-1, keepdims=True)
    h2 = (y1 * lax.rsqrt(ms + EPS)) * g2_ref[...]
    hh = h2.astype(BF16)
    hb = hh.astype(F32)
    h2_ref[...] = hb
    hl = (h2 - hb).astype(BF16)
    zz = _dot(hh, wr_ref[...])
    z = zz[:, :LANES] + zz[:, LANES:] + _dot(hl, wr_ref[:, :LANES]) + br_ref[...]

    rr = lax.broadcasted_iota(jnp.int32, (tm, tm), 0)
    cc = lax.broadcasted_iota(jnp.int32, (tm, tm), 1)
    earlier = jnp.where(cc < rr, 1.0, 0.0).astype(BF16)

    def prefix_of(one_hot):
        return _dot(earlier, one_hot.astype(BF16)) + cnt_s[...]

    rec, one_hot = _route(z, prefix_of)
    rec_ref[...] = rec
    rect_ref[...] = rec.T[:R_FIELDS, :]
    cnt_s[...] += jnp.sum(one_hot, axis=0, keepdims=True)
    cnt_ref[...] = cnt_s[...]


def _outproj(ao, go, x, woa, wog, g2, wr, br):
    n = x.shape[0]
    tm = min(512, n)
    row = lambda w: pl.BlockSpec((tm, w), lambda i: (i, 0))
    full = lambda a: pl.BlockSpec(a.shape, lambda i: (0, 0))
    return pl.pallas_call(
        _outproj_kernel,
        grid=(n // tm,),
        in_specs=[row(A_WIDTH), row(G_WIDTH), row(D_MODEL), full(woa), full(wog), full(g2), full(wr), full(br)],
        out_specs=[row(D_MODEL), row(D_MODEL), row(LANES),
                   pl.BlockSpec((R_FIELDS, tm), lambda i: (0, i)), pl.BlockSpec((1, LANES), lambda i: (0, 0))],
        out_shape=[jax.ShapeDtypeStruct((n, D_MODEL), F32), jax.ShapeDtypeStruct((n, D_MODEL), F32),
                   jax.ShapeDtypeStruct((n, LANES), F32), jax.ShapeDtypeStruct((R_FIELDS, n), F32),
                   jax.ShapeDtypeStruct((1, LANES), F32)],
        scratch_shapes=[pltpu.VMEM((1, LANES), F32)],
        compiler_params=_cparams(("arbitrary",)),
        name="outproj_router",
    )(ao, go, x, woa, wog, g2, wr, br)


def _row(ref, r):
    return ref.at[pl.ds(r, 1), :]


def _dispatch_kernel(pos1_ref, pos2_ref, h_ref, xs_ref, sem):
    tm = h_ref.shape[0]

    def copies(t):
        return [pltpu.make_async_copy(_row(h_ref, t), _row(xs_ref, p[0, t]), sem) for p in (pos1_ref, pos2_ref)]

    def start(t, c):
        for cp in copies(t):
            cp.start()
        return c

    def wait(t, c):
        for cp in copies(t):
            cp.wait()
        return c

    lax.fori_loop(0, tm, start, 0, unroll=MOE_DMA_UNROLL)
    lax.fori_loop(0, tm, wait, 0, unroll=MOE_DMA_UNROLL)


def _dispatch(pos1, pos2, h2, *, tm):
    n = h2.shape[0]
    idx = pl.BlockSpec((None, 1, tm), lambda i: (i, 0, 0), memory_space=pltpu.SMEM)
    return pl.pallas_call(
        _dispatch_kernel,
        grid=(n // tm,),
        in_specs=[idx, idx, pl.BlockSpec((tm, D_MODEL), lambda i: (i, 0))],
        out_specs=pl.BlockSpec(memory_space=pl.ANY),
        out_shape=jax.ShapeDtypeStruct((2 * n, D_MODEL), F32),
        scratch_shapes=[pltpu.SemaphoreType.DMA(())],
        compiler_params=_cparams(("arbitrary",)),
        name="moe_dispatch",
    )(pos1, pos2, h2)


def _experts_kernel(vt_ref, ve_ref, vlo_ref, vhi_ref, vfirst_ref, xs_ref, wg_ref, wu_ref, wd_ref, ys_ref,
                    wg_s, wu_s, wd_s):
    v = pl.program_id(0)
    t = xs_ref.shape[0]
    rg = min(MOE_ROW_GROUP, t)
    groups = range(t // rg)

    @pl.when(jnp.logical_or(v == 0, ve_ref[v] != ve_ref[jnp.maximum(v - 1, 0)]))
    def _():
        wg_s[...] = wg_ref[...].astype(BF16)
        wu_s[...] = wu_ref[...].astype(BF16)
        wd_s[...] = wd_ref[...].astype(BF16)

    wg = wg_s[...]
    wu = wu_s[...]
    wd = wd_s[...]
    outs = []
    for g in groups:
        x = xs_ref[g * rg:(g + 1) * rg, :].astype(BF16)
        gate = _dot(x, wg)
        a = ((gate * jax.nn.sigmoid(gate)) * _dot(x, wu)).astype(BF16)
        row = lax.broadcasted_iota(jnp.int32, (rg, D_MODEL), 0) + g * rg
        outs.append((_dot(a, wd), jnp.logical_and(row >= vlo_ref[v], row < vhi_ref[v])))

    @pl.when(vfirst_ref[v] == 1)
    def _():
        for g in groups:
            d, mine = outs[g]
            ys_ref[g * rg:(g + 1) * rg, :] = jnp.where(mine, d, 0.0)

    @pl.when(vfirst_ref[v] == 0)
    def _():
        for g in groups:
            d, mine = outs[g]
            ys_ref[g * rg:(g + 1) * rg, :] = jnp.where(mine, d, ys_ref[g * rg:(g + 1) * rg, :])


def _experts(visits, xs, w_gate, w_up, w_down, *, t):
    nv = visits[0].shape[0]
    slot = pl.BlockSpec((t, D_MODEL), lambda v, vt, ve, lo, hi, fi: (vt[v], 0))
    w_in = pl.BlockSpec((None, D_MODEL, D_EXPERT), lambda v, vt, ve, lo, hi, fi: (ve[v], 0, 0))
    w_out = pl.BlockSpec((None, D_EXPERT, D_MODEL), lambda v, vt, ve, lo, hi, fi: (ve[v], 0, 0))
    return pl.pallas_call(
        _experts_kernel,
        grid_spec=pltpu.PrefetchScalarGridSpec(
            num_scalar_prefetch=5, grid=(nv,), in_specs=[slot, w_in, w_in, w_out], out_specs=slot,
            scratch_shapes=[pltpu.VMEM((D_MODEL, D_EXPERT), BF16), pltpu.VMEM((D_MODEL, D_EXPERT), BF16),
                            pltpu.VMEM((D_EXPERT, D_MODEL), BF16)]),
        out_shape=jax.ShapeDtypeStruct(xs.shape, F32),
        compiler_params=_cparams(("arbitrary",)),
        name="moe_experts",
    )(*visits, xs, w_gate, w_up, w_down)


def _combine_kernel(pos1_ref, pos2_ref, y1_ref, rec_ref, ys_ref, out_ref, g1_s, g2_s, sem):
    tm = y1_ref.shape[0]

    def copies(t):
        return [pltpu.make_async_copy(_row(ys_ref, p[0, t]), _row(g, t), sem)
                for p, g in ((pos1_ref, g1_s), (pos2_ref, g2_s))]

    def start(t, c):
        for cp in copies(t):
            cp.start()
        return c

    def wait(t, c):
        for cp in copies(t):
            cp.wait()
        return c

    lax.fori_loop(0, tm, start, 0, unroll=MOE_DMA_UNROLL)
    rec = rec_ref[...]
    w1 = rec[:, R_W1:R_W1 + 1]
    w2 = rec[:, R_W2:R_W2 + 1]
    lax.fori_loop(0, tm, wait, 0, unroll=MOE_DMA_UNROLL)
    out_ref[...] = y1_ref[...] + w1 * g1_s[...] + w2 * g2_s[...]


def _combine(pos1, pos2, y1, rec, ys, *, tm):
    n = y1.shape[0]
    idx = pl.BlockSpec((None, 1, tm), lambda i: (i, 0, 0), memory_space=pltpu.SMEM)
    row = lambda w: pl.BlockSpec((tm, w), lambda i: (i, 0))
    return pl.pallas_call(
        _combine_kernel,
        grid=(n // tm,),
        in_specs=[idx, idx, row(D_MODEL), row(LANES), pl.BlockSpec(memory_space=pl.ANY)],
        out_specs=row(D_MODEL),
        out_shape=jax.ShapeDtypeStruct((n, D_MODEL), F32),
        scratch_shapes=[pltpu.VMEM((tm, D_MODEL), F32), pltpu.VMEM((tm, D_MODEL), F32),
                        pltpu.SemaphoreType.DMA(())],
        compiler_params=_cparams(("arbitrary",)),
        name="moe_combine",
    )(pos1, pos2, y1, rec, ys)


def _expert_visits(counts, n_slots, t):
    n_tiles = n_slots // t
    nv = n_tiles + N_EXPERTS - 1
    end = jnp.cumsum(counts)
    start = end - counts
    first_tile = start // t
    last_tile = jnp.maximum(end - 1, start) // t
    n_vis = jnp.where(counts > 0, last_tile - first_tile + 1, 0)
    vis_end = jnp.cumsum(n_vis)
    total = vis_end[-1]
    v = jnp.arange(nv, dtype=jnp.int32)
    e = jnp.minimum(jnp.sum((vis_end[None, :] <= v[:, None]).astype(jnp.int32), axis=1), N_EXPERTS - 1)
    pick = lambda a: jnp.sum(jnp.where(e[:, None] == jnp.arange(N_EXPERTS)[None, :], a[None, :], 0), axis=1)
    tile = pick(first_tile) + (v - (pick(vis_end) - pick(n_vis)))
    lo = jnp.maximum(pick(start), tile * t) - tile * t
    hi = jnp.minimum(pick(end), (tile + 1) * t) - tile * t
    live = v < total
    last_e = jnp.max(jnp.where(counts > 0, jnp.arange(N_EXPERTS), 0))
    tile = jnp.where(live, tile, n_tiles - 1)
    e = jnp.where(live, e, last_e)
    lo = jnp.where(live, lo, 0)
    hi = jnp.where(live, hi, 0)
    first = jnp.concatenate([jnp.ones((1,), jnp.int32), (tile[1:] != tile[:-1]).astype(jnp.int32)])
    return tuple(a.astype(jnp.int32) for a in (tile, e, lo, hi, first))


def _moe(h2, y1, rec, rect, cnt, w_gate, w_up, w_down):
    n = y1.shape[0]
    tm = min(MOE_TM, n)
    t = min(MOE_SLOT_TILE, 2 * n)
    counts = cnt[0, ROUTE_OFF:ROUTE_OFF + N_EXPERTS].astype(jnp.int32)
    start = jnp.cumsum(counts) - counts
    ids = rect.astype(jnp.int32)
    experts = jnp.arange(N_EXPERTS, dtype=jnp.int32)[:, None]

    def slot(e_row, rank_row):
        return (jnp.sum(jnp.where(ids[e_row][None, :] == experts, start[:, None], 0), axis=0)
                + ids[rank_row]).reshape(n // tm, 1, tm)

    pos1 = slot(R_E1, R_RANK1)
    pos2 = slot(R_E2, R_RANK2)
    xs = _dispatch(pos1, pos2, h2, tm=tm)
    ys = _experts(_expert_visits(counts, 2 * n, t), xs, w_gate, w_up, w_down, t=t)
    return _combine(pos1, pos2, y1, rec, ys, tm=tm)


def _layer_weights(l, norm1_gain, w_in, a_q_gain, a_k_gain, lambda_q1, lambda_k1, lambda_q2, lambda_k2,
                   a_out_gain, w_a2, b_a, g_out_gain, w_out, norm2_gain, w_group, b_group, w_erouter,
                   b_erouter, w_gate, w_up, w_down):
    w = w_in[l]
    w_la = _fold(w[:, _C_GA:_C_GR], w_a2[l])
    qk_cols = []
    for h in range(G_HEADS):
        qk_cols += [w[:, _C_GQ + h * G_DK:_C_GQ + (h + 1) * G_DK], w[:, _C_GK + h * G_DK:_C_GK + (h + 1) * G_DK]]
    w_big = jnp.concatenate([w[:, :_C_GQ]] + qk_cols + [w[:, _C_GV:_C_GA], w[:, _C_GR:], w_la], axis=1).astype(BF16)
    w_r = jnp.concatenate([w_group[l], w_erouter[l],
                           jnp.zeros((D_MODEL, LANES - N_GROUPS - N_EXPERTS), F32)], axis=1)
    w_rh = w_r.astype(BF16)
    w_r2 = jnp.concatenate([w_rh, (w_r - w_rh.astype(F32)).astype(BF16)], axis=1)
    b_r = jnp.concatenate([b_group[l], b_erouter[l], jnp.zeros((LANES - N_GROUPS - N_EXPERTS,), F32)])[None, :]
    return dict(
        g1=norm1_gain[l][None, :], w_big=w_big,
        qg2=jnp.tile(a_q_gain[l], 2)[None, :], kg2=jnp.tile(a_k_gain[l], 2)[None, :],
        ba=b_a[l][None, :],
        lams=(lambda_q1[l][None, :], lambda_k1[l][None, :], lambda_q2[l][None, :], lambda_k2[l][None, :]),
        a_gain=a_out_gain[l][None, :], g_gain=g_out_gain[l][None, :],
        woa=w_out[l][:A_WIDTH].astype(BF16), wog=w_out[l][A_WIDTH:].astype(BF16),
        g2=norm2_gain[l][None, :], w_r2=w_r2, b_r=b_r,
        w_gate=w_gate[l], w_up=w_up[l], w_down=w_down[l],
    )


def _mix_and_moe(x, p, *, batch, seq, chunk, lam_init, cache=None, state=None):
    prompt = cache is None
    res = _inproj(x, p["g1"], p["w_big"], p["qg2"], p["kg2"], p["ba"], batch=batch, seq=seq, final_layout=prompt)
    if prompt:
        q, kt, ktb, v4, vb, qk, gv, gr, la = res
        ao = _attn_prompt(p["lams"], p["a_gain"], q, ktb, vb, batch=batch, seq=seq, lam_init=lam_init)
        s0 = jnp.zeros((batch, G_HEADS, G_DK, G_DV), F32)
        k_out = jnp.transpose(kt.reshape(batch, A_HEADS, 2, A_DH, seq), (0, 4, 1, 2, 3))
        v_out = v4.reshape(batch, seq, A_HEADS, A_DV)
    else:
        q, k, v, qk, gv, gr, la = res
        kct, vc = cache
        ao = _attn_sample(p["lams"], p["a_gain"], q, k, v, kct, vc, batch=batch, seq=seq,
                          past=kct.shape[-1], lam_init=lam_init)
        s0 = state
        k_out = k.reshape(batch, seq, A_HEADS, 2, A_DH)
        v_out = v.reshape(batch, seq, A_HEADS, A_DV)
    go, st = _gla(qk, la, gv, gr, p["g_gain"], s0, batch=batch, seq=seq, chunk=chunk)
    y1, h2, rec, rect, cnt = _outproj(ao, go, x, p["woa"], p["wog"], p["g2"], p["w_r2"], p["b_r"])
    y = _moe(h2, y1, rec, rect, cnt, p["w_gate"], p["w_up"], p["w_down"])
    return y, k_out, v_out, st


def kernel(x_prompt, x_sample, cache_k, cache_v, state_gla, norm1_gain, w_in, a_q_gain, a_k_gain, lambda_q1, lambda_k1, lambda_q2, lambda_k2, a_out_gain, w_a2, b_a, g_out_gain, w_out, norm2_gain, w_group, b_group, w_erouter, b_erouter, w_gate, w_up, w_down):
    weights = (norm1_gain, w_in, a_q_gain, a_k_gain, lambda_q1, lambda_k1, lambda_q2, lambda_k2, a_out_gain,
               w_a2, b_a, g_out_gain, w_out, norm2_gain, w_group, b_group, w_erouter, b_erouter, w_gate,
               w_up, w_down)
    depth = w_in.shape[0]
    pb, pl_, d = x_prompt.shape
    sb, sl, _ = x_sample.shape
    past = cache_k.shape[2]
    y_p = x_prompt.reshape(pb * pl_, d)
    y_s = x_sample.reshape(sb * sl, d)
    outs = [[] for _ in range(6)]
    for l in range(depth):
        lam_init = 0.8 - 0.6 * math.exp(-0.3 * l)
        p = _layer_weights(l, *weights)
        y_p, kp, vp, sp = _mix_and_moe(y_p, p, batch=pb, seq=pl_, chunk=CHUNK, lam_init=lam_init)
        cache = (jnp.transpose(cache_k[l], (0, 2, 3, 4, 1)).reshape(sb, A_WIDTH, past),
                 cache_v[l].reshape(sb * past * A_HEADS, A_DV))
        y_s, kn, vn, sn = _mix_and_moe(y_s, p, batch=sb, seq=sl, chunk=sl, lam_init=lam_init,
                                       cache=cache, state=state_gla[l])
        for o, t in zip(outs, (kp, vp, sp, kn, vn, sn)):
            o.append(t)
    return (y_p.reshape(pb, pl_, d), y_s.reshape(sb, sl, d)) + tuple(jnp.stack(o) for o in outs)
```

```python
import functools
import math

import jax
import jax.numpy as jnp
from jax import lax
from jax.experimental import pallas as pl
from jax.experimental.pallas import tpu as pltpu

F32 = jnp.float32
BF16 = jnp.bfloat16

D_MODEL = 1024
CHUNK = 64
A_HEADS = 4
A_DH = 64
A_DV = 128
A_WIDTH = A_HEADS * A_DV
G_HEADS = 4
G_DK = 64
G_DV = 128
G_WIDTH = G_HEADS * G_DV
G_RANK = 16
G_TAU = 16.0
N_GROUPS = 4
EXPERTS_PER_GROUP = 8
N_EXPERTS = N_GROUPS * EXPERTS_PER_GROUP
D_EXPERT = D_MODEL // 4
EPS = 1e-6

LANES = 128
NEG = -1e30
VMEM_LIMIT = 56 * 1024 * 1024

_C_AQ, _C_AK, _C_AV = 0, 512, 1024
_C_GQ, _C_GK, _C_GV = 1536, 1792, 2048
_C_GA, _C_GR = 2560, 2576
W_BIG = 3328

INPROJ_TM = 512
ATT_TQ = 1024
ATT_TK = 1024
ATT_RG = 256
LOG2E = 1.4426950408889634
ATT_SAFE_BOUND = 43.0 * LOG2E
GLA_UNROLL = 8
ROUTE_OFF = N_GROUPS
MOE_TM = 1024
MOE_DMA_UNROLL = 8
MOE_SLOT_TILE = 512
MOE_ROW_GROUP = 256


def _cparams(sem):
    return pltpu.CompilerParams(dimension_semantics=sem, vmem_limit_bytes=VMEM_LIMIT)


def _nt(a, b):
    return lax.dot_general(a, b, (((1,), (1,)), ((), ())), preferred_element_type=F32)


def _tn(a, b):
    return lax.dot_general(a, b, (((0,), (0,)), ((), ())), preferred_element_type=F32)


def _dot(a, b):
    return jnp.dot(a, b, preferred_element_type=F32)


def _fold_kernel(wga_ref, wa2_ref, out_ref):
    out_ref[...] = jnp.dot(wga_ref[...], wa2_ref[...], preferred_element_type=F32,
                           precision=lax.Precision.HIGHEST)


def _fold(w_ga, w_a2):
    return pl.pallas_call(
        _fold_kernel,
        out_shape=jax.ShapeDtypeStruct((D_MODEL, G_HEADS * G_DK), F32),
        name="fold_gate",
    )(w_ga, w_a2)


def _headnorm(z, gain2):
    outs = []
    lane = lax.broadcasted_iota(jnp.int32, (z.shape[0], LANES), 1)
    lo = lane < A_DH
    for c in range(z.shape[1] // LANES):
        x = z[:, c * LANES:(c + 1) * LANES]
        xx = x * x
        s_lo = jnp.sum(jnp.where(lo, xx, 0.0), axis=-1, keepdims=True)
        s_hi = jnp.sum(jnp.where(lo, 0.0, xx), axis=-1, keepdims=True)
        r = jnp.where(lo, lax.rsqrt(s_lo * (1.0 / A_DH) + EPS), lax.rsqrt(s_hi * (1.0 / A_DH) + EPS))
        outs.append((x * r) * gain2)
    return outs


def _inproj_kernel(x_ref, g1_ref, w_ref, qg_ref, kg_ref, ba_ref, *out_refs, final_layout):
    if final_layout:
        q_ref, kt_ref, ktb_ref, v4_ref, vb_ref, qk_ref, gv_ref, gr_ref, la_ref = out_refs
    else:
        q_ref, k_ref, v_ref, qk_ref, gv_ref, gr_ref, la_ref = out_refs
    x = x_ref[...]
    tm = x.shape[0]
    ms = jnp.mean(x * x, axis=-1, keepdims=True)
    h = ((x * lax.rsqrt(ms + EPS)) * g1_ref[...]).astype(BF16)

    def seg(lo, hi):
        return _dot(h, w_ref[:, lo:hi])

    for c, y in enumerate(_headnorm(seg(0, 512), qg_ref[...])):
        q_ref[:, c * LANES:(c + 1) * LANES] = (y * (A_DH ** -0.5 * LOG2E)).astype(BF16)
    for c, y in enumerate(_headnorm(seg(512, 1024), kg_ref[...])):
        cols = slice(c * LANES, (c + 1) * LANES)
        if final_layout:
            yt = y.T
            kt_ref[cols, :] = yt
            ktb_ref[cols, :] = yt.astype(BF16)
        else:
            k_ref[:, cols] = y
    v = seg(1024, 1536)
    if final_layout:
        vb_ref[...] = v.astype(BF16)
        for c in range(A_HEADS):
            v4_ref[pl.ds(c, tm, stride=A_HEADS), :] = v[:, c * LANES:(c + 1) * LANES]
    else:
        v_ref[...] = v
    qk_ref[...] = seg(1536, 2048)
    gv_ref[...] = seg(2048, 2560)
    gr_ref[...] = seg(2560, 3072)
    zl = seg(3072, 3328) + ba_ref[...]
    la_ref[...] = (jnp.minimum(zl, 0.0) - jnp.log1p(jnp.exp(-jnp.abs(zl)))) * (1.0 / G_TAU)


def _inproj(x, g1, w_big, qg2, kg2, ba, *, batch, seq, final_layout):
    n = x.shape[0]
    tm = min(INPROJ_TM, seq)
    nlt = seq // tm
    row = lambda w: pl.BlockSpec((tm, w), lambda i: (i, 0))
    full = lambda a: pl.BlockSpec(a.shape, lambda i: (0, 0))
    tail = [(row(512), (n, 512), F32)] * 3 + [(row(256), (n, 256), F32)]
    if final_layout:
        outs = [
            (row(512), (n, 512), BF16),
            (pl.BlockSpec((None, 512, tm), lambda i: (i // nlt, 0, i % nlt)), (batch, 512, seq), F32),
            (pl.BlockSpec((None, None, 512, tm), lambda i: (i // nlt, i % nlt, 0, 0)),
             (batch, nlt, 512, tm), BF16),
            (pl.BlockSpec((tm * A_HEADS, LANES), lambda i: (i, 0)), (n * A_HEADS, LANES), F32),
            (row(512), (n, 512), BF16),
        ] + tail
    else:
        outs = [(row(512), (n, 512), BF16), (row(512), (n, 512), F32), (row(512), (n, 512), F32)] + tail
    return pl.pallas_call(
        functools.partial(_inproj_kernel, final_layout=final_layout),
        grid=(n // tm,),
        in_specs=[row(D_MODEL), full(g1), full(w_big), full(qg2), full(kg2), full(ba)],
        out_specs=[o[0] for o in outs],
        out_shape=[jax.ShapeDtypeStruct(o[1], o[2]) for o in outs],
        compiler_params=_cparams(("parallel",)),
        name="inproj",
    )(x, g1, w_big, qg2, kg2, ba)


def _diff_lambda(lq1_ref, lk1_ref, lq2_ref, lk2_ref, lam_init):
    a = jnp.sum(lq1_ref[...] * lk1_ref[...], axis=-1, keepdims=True)
    b = jnp.sum(lq2_ref[...] * lk2_ref[...], axis=-1, keepdims=True)
    return jnp.exp(a) - jnp.exp(b) + lam_init


def _attn_prompt_kernel(lq1_ref, lk1_ref, lq2_ref, lk2_ref, gain_ref, q_ref, kt_ref, v_ref,
                        o_ref, acc_s, *, seq, lam_init):
    tq, tk, rg = ATT_TQ, ATT_TK, ATT_RG
    kb = kt_ref.shape[-1]
    parts = tq // rg
    nrg = 2 * parts
    lam = _diff_lambda(lq1_ref, lk1_ref, lq2_ref, lk2_ref, lam_init)
    lane = lax.broadcasted_iota(jnp.int32, (tq, LANES), 1)
    lo = lane < A_DH
    ones = jnp.ones((tk, LANES), BF16)
    unit = jnp.where(lax.broadcasted_iota(jnp.int32, (LANES, tk), 0) == 0, 1.0, 0.0).astype(BF16)

    def visible(part, nk):
        qq = lax.broadcasted_iota(jnp.int32, (rg, nk), 0) + part * rg
        kk = lax.broadcasted_iota(jnp.int32, (rg, nk), 1)
        return (kk // CHUNK) <= (qq // CHUNK)

    def key_norms(j, c):
        k = kt_ref[j].astype(F32)
        sq = k * k
        n1 = jnp.max(jnp.sum(sq[:A_DH], axis=0, keepdims=True), axis=1, keepdims=True)
        n2 = jnp.max(jnp.sum(sq[A_DH:], axis=0, keepdims=True), axis=1, keepdims=True)
        return jnp.maximum(c[0], n1), jnp.maximum(c[1], n2)

    zero11 = jnp.zeros((1, 1), F32)
    k1sq, k2sq = lax.fori_loop(0, seq // kb, key_norms, (zero11, zero11))

    def keys(j):
        return jnp.concatenate([kt_ref[j * (tk // kb) + b] for b in range(tk // kb)], axis=1)

    def values(j):
        vb = v_ref[pl.ds(pl.multiple_of(j * tk, tk), tk), :]
        return jnp.concatenate([vb, ones], axis=1)

    def shifted_step(j, carry, qx, diagonal):
        ktx = jnp.concatenate([keys(j), unit], axis=0)
        vx = values(j)
        for g in range(nrg):
            rows = slice(g * rg, (g + 1) * rg)
            part = g % parts
            nk = (part + 1) * rg if diagonal else tk
            p = jnp.exp2(_dot(qx[rows], ktx[:, :nk]))
            if diagonal:
                p = jnp.where(visible(part, nk), p, 0.0)
            acc_s[rows, :] += _dot(p.astype(BF16), vx[:nk])
        return carry

    def running_max_step(j, ms, qz, diagonal):
        kt = keys(j)
        vx = values(j)
        out = []
        for g in range(nrg):
            rows = slice(g * rg, (g + 1) * rg)
            s = _dot(qz[rows], kt)
            if diagonal:
                s = jnp.where(visible(g % parts, tk), s, NEG)
            m_new = jnp.maximum(ms[g], jnp.max(s, axis=-1, keepdims=True))
            alpha = jnp.exp2(ms[g] - m_new)
            p = jnp.exp2(s - m_new).astype(BF16)
            acc_s[rows, :] = alpha * acc_s[rows, :] + _dot(p, vx)
            out.append(m_new)
        return tuple(out)

    def q_block(i, carry):
        qrows = pl.ds(pl.multiple_of(i * tq, tq), tq)
        qi = q_ref[qrows, :]
        zero = jnp.zeros_like(qi)
        q1 = jnp.where(lo, qi, zero)
        q2 = jnp.where(lo, zero, qi)
        qf = qi.astype(F32)
        sq = qf * qf
        shift1 = jnp.sqrt(jnp.sum(jnp.where(lo, sq, 0.0), axis=-1, keepdims=True) * k1sq)
        shift2 = jnp.sqrt(jnp.sum(jnp.where(lo, 0.0, sq), axis=-1, keepdims=True) * k2sq)
        safe = jnp.max(jnp.maximum(shift1, shift2)) <= ATT_SAFE_BOUND
        acc_s[...] = jnp.zeros_like(acc_s)

        @pl.when(safe)
        def _():
            x1 = jnp.where(lane == 0, -shift1, 0.0).astype(BF16)
            x2 = jnp.where(lane == 0, -shift2, 0.0).astype(BF16)
            qx = jnp.concatenate([jnp.concatenate([q1, x1], axis=1), jnp.concatenate([q2, x2], axis=1)], axis=0)
            lax.fori_loop(0, i, lambda j, c: shifted_step(j, c, qx, False), 0)
            shifted_step(i, 0, qx, True)

        @pl.when(jnp.logical_not(safe))
        def _():
            qz = jnp.concatenate([q1, q2], axis=0)
            ms = (jnp.full((rg, 1), NEG, F32),) * nrg
            ms = lax.fori_loop(0, i, lambda j, c: running_max_step(j, c, qz, False), ms)
            running_max_step(i, ms, qz, True)

        a1 = acc_s[:tq, :]
        a2 = acc_s[tq:, :]
        o = a1[:, :A_DV] / a1[:, A_DV:] - lam * (a2[:, :A_DV] / a2[:, A_DV:])
        msq = jnp.mean(o * o, axis=-1, keepdims=True)
        o_ref[qrows, :] = (((o * lax.rsqrt(msq + EPS)) * gain_ref[...]) * (1.0 - lam_init)).astype(o_ref.dtype)
        return carry

    lax.fori_loop(0, seq // tq, q_block, 0)


def _attn_prompt(lams, gain_row, q, ktb, v, *, batch, seq, lam_init):
    nkb, kb = ktb.shape[1], ktb.shape[3]
    assert seq % ATT_TQ == 0 and ATT_TQ == ATT_TK and ATT_TK % kb == 0, (seq, kb)
    vec = pl.BlockSpec((1, A_DH), lambda b, h: (0, 0))
    head = pl.BlockSpec((seq, LANES), lambda b, h: (b, h))
    return pl.pallas_call(
        functools.partial(_attn_prompt_kernel, seq=seq, lam_init=lam_init),
        grid=(batch, A_HEADS),
        in_specs=[vec, vec, vec, vec, pl.BlockSpec((1, A_DV), lambda b, h: (0, 0)), head,
                  pl.BlockSpec((None, nkb, LANES, kb), lambda b, h: (b, 0, h, 0)), head],
        out_specs=head,
        out_shape=jax.ShapeDtypeStruct((batch * seq, A_WIDTH), BF16),
        scratch_shapes=[pltpu.VMEM((2 * ATT_TQ, A_DV + LANES), F32)],
        compiler_params=_cparams(("parallel", "parallel")),
        name="attn_prompt",
    )(*lams, gain_row, q, ktb, v)


def _attn_sample_kernel(lq1_ref, lk1_ref, lq2_ref, lk2_ref, gain_ref, q_ref, kn_ref, vn_ref,
                        kct_ref, vc_ref, o_ref, *, past, lam_init):
    lam = _diff_lambda(lq1_ref, lk1_ref, lq2_ref, lk2_ref, lam_init)
    nq = q_ref.shape[0]
    lane = lax.broadcasted_iota(jnp.int32, (nq, LANES), 1)
    for h in range(A_HEADS):
        cols = slice(h * LANES, (h + 1) * LANES)
        q = q_ref[:, cols]
        zero = jnp.zeros_like(q)
        qz = jnp.concatenate([jnp.where(lane < A_DH, q, zero), jnp.where(lane < A_DH, zero, q)], axis=0)
        vc = vc_ref[pl.ds(h, past, stride=A_HEADS), :].astype(BF16)
        sc = _dot(qz, kct_ref[cols, :].astype(BF16))
        sn = _nt(qz, kn_ref[:, cols].astype(BF16))
        m = jnp.maximum(jnp.max(sc, axis=-1, keepdims=True), jnp.max(sn, axis=-1, keepdims=True))
        ec = jnp.exp2(sc - m)
        en = jnp.exp2(sn - m)
        l = jnp.sum(ec, axis=-1, keepdims=True) + jnp.sum(en, axis=-1, keepdims=True)
        pv = _dot(ec.astype(BF16), vc) + _dot(en.astype(BF16), vn_ref[:, cols].astype(BF16))
        pv = pv * (1.0 / l)
        o = pv[:nq] - lam * pv[nq:]
        ms = jnp.mean(o * o, axis=-1, keepdims=True)
        o_ref[:, cols] = (((o * lax.rsqrt(ms + EPS)) * gain_ref[...]) * (1.0 - lam_init)).astype(o_ref.dtype)


def _attn_sample(lams, gain_row, q, k, v, kct, vc, *, batch, seq, past, lam_init):
    vec = pl.BlockSpec((1, A_DH), lambda b: (0, 0))
    new = pl.BlockSpec((seq, A_WIDTH), lambda b: (b, 0))
    return pl.pallas_call(
        functools.partial(_attn_sample_kernel, past=past, lam_init=lam_init),
        grid=(batch,),
        in_specs=[vec, vec, vec, vec, pl.BlockSpec((1, A_DV), lambda b: (0, 0)), new, new, new,
                  pl.BlockSpec((None, A_WIDTH, past), lambda b: (b, 0, 0)),
                  pl.BlockSpec((past * A_HEADS, LANES), lambda b: (b, 0))],
        out_specs=new,
        out_shape=jax.ShapeDtypeStruct((batch * seq, A_WIDTH), BF16),
        compiler_params=_cparams(("parallel",)),
        name="attn_sample",
    )(*lams, gain_row, q, k, v, kct, vc)


def _cumsum_rows(x):
    n = x.shape[0]
    row = lax.broadcasted_iota(jnp.int32, x.shape, 0)
    s = 1
    while s < n:
        x = x + jnp.where(row >= s, pltpu.roll(x, s, 0), 0.0)
        s *= 2
    return x


def _gla_kernel(qk_ref, la_ref, v_ref, gr_ref, gain_ref, s0_ref, go_ref, st_ref, st_s, *, chunk, rows_per_step):
    c = chunk
    n_chunks = rows_per_step // c
    unroll = GLA_UNROLL if n_chunks % GLA_UNROLL == 0 else 1

    @pl.when(pl.program_id(1) == 0)
    def _():
        st_s[...] = s0_ref[...]

    lane = lax.broadcasted_iota(jnp.int32, (c, LANES), 1)
    lo = lane < G_DK
    causal = lax.broadcasted_iota(jnp.int32, (c, c), 0) >= lax.broadcasted_iota(jnp.int32, (c, c), 1)
    diag = lax.broadcasted_iota(jnp.int32, (G_DK, LANES), 0) == lax.broadcasted_iota(jnp.int32, (G_DK, LANES), 1)
    scale = G_DK ** -0.5
    pad_lanes = jnp.zeros((c, LANES - c), F32)
    pad_rows = [jnp.zeros((G_DK - c, G_DV), BF16)] if c < G_DK else []

    def one_chunk(ci):
        rows = pl.ds(pl.multiple_of(ci * c, c), c)
        bsum = _cumsum_rows(la_ref[rows, :])
        for h in range(G_HEADS):
            cols = slice(h * LANES, (h + 1) * LANES)
            qk = qk_ref[rows, cols]
            kq = pltpu.roll(qk, G_DK, 1)
            bg = bsum[:, (h // 2) * LANES:(h // 2 + 1) * LANES]
            br = pltpu.roll(bg, G_DK, 1)
            b = jnp.where(lo, bg, br) if h % 2 == 0 else jnp.where(lo, br, bg)
            b_last = b[c - 1:c, :]
            b_mid = b[c // 2 - 1:c // 2, :]
            qt = (qk * jnp.exp(b - b_mid)) * scale
            kt = kq * jnp.exp(b_mid - b)
            qe = (kq * jnp.exp(b)) * scale
            kh = kq * jnp.exp(b_last - b)
            sc = jnp.where(causal, _nt(qt[:, :G_DK].astype(BF16), kt[:, :G_DK].astype(BF16)), 0.0)
            lhs = jnp.where(lo, jnp.concatenate([sc, pad_lanes], axis=1), qe).astype(BF16)
            vb = v_ref[rows, cols].astype(BF16)
            st = st_s[h]
            o = _dot(lhs, jnp.concatenate([vb] + pad_rows + [st.astype(BF16)], axis=0))
            decay = jnp.sum(jnp.where(diag, jnp.exp(b_last), 0.0), axis=-1, keepdims=True)
            st_s[h] = decay * st + _tn(kh[:, :G_DK].astype(BF16), vb)
            ms = jnp.mean(o * o, axis=-1, keepdims=True)
            on = (o * lax.rsqrt(ms + EPS)) * gain_ref[...]
            g = gr_ref[rows, cols]
            go_ref[rows, cols] = (on * (g * jax.nn.sigmoid(g))).astype(go_ref.dtype)

    def chunk_step(i, carry):
        for u in range(unroll):
            one_chunk(i * unroll + u)
        return carry

    lax.fori_loop(0, n_chunks // unroll, chunk_step, 0)
    st_ref[...] = st_s[...]


def _gla(qk, la, v, gr, gain_row, s0, *, batch, seq, chunk):
    lb = min(seq, 1024)
    nl = seq // lb
    wide = pl.BlockSpec((lb, G_WIDTH), lambda b, l: (b * nl + l, 0))
    state = pl.BlockSpec((None, G_HEADS, G_DK, G_DV), lambda b, l: (b, 0, 0, 0))
    return pl.pallas_call(
        functools.partial(_gla_kernel, chunk=chunk, rows_per_step=lb),
        grid=(batch, nl),
        in_specs=[wide, pl.BlockSpec((lb, G_HEADS * G_DK), lambda b, l: (b * nl + l, 0)), wide, wide,
                  pl.BlockSpec((1, G_DV), lambda b, l: (0, 0)), state],
        out_specs=[wide, state],
        out_shape=[jax.ShapeDtypeStruct((batch * seq, G_WIDTH), BF16),
                   jax.ShapeDtypeStruct((batch, G_HEADS, G_DK, G_DV), F32)],
        scratch_shapes=[pltpu.VMEM((G_HEADS, G_DK, G_DV), F32)],
        compiler_params=_cparams(("arbitrary", "arbitrary")),
        name="gla",
    )(qk, la, v, gr, gain_row, s0)


R_E1, R_E2, R_W1, R_W2, R_RANK1, R_RANK2, R_FIELDS = 0, 1, 2, 3, 4, 5, 8
ROUTE_ROWS = 40


def _route(zt, prefix_of):
    row = lax.broadcasted_iota(jnp.int32, zt.shape, 0)
    big = jnp.int32(LANES)

    def first_argmax(vals, vmax):
        return jnp.min(jnp.where(vals == vmax, row, big), axis=0, keepdims=True)

    zg = jnp.where(row < N_GROUPS, zt, NEG)
    gmax = jnp.max(zg, axis=0, keepdims=True)
    g_idx = first_argmax(zg, gmax)
    g_w = 1.0 / jnp.sum(jnp.exp(zg - gmax), axis=0, keepdims=True)
    first = ROUTE_OFF + EXPERTS_PER_GROUP * g_idx
    ze = jnp.where(row < first, NEG, jnp.where(row < first + EXPERTS_PER_GROUP, zt, NEG))
    v1 = jnp.max(ze, axis=0, keepdims=True)
    i1 = first_argmax(ze, v1)
    ze2 = jnp.where(row == i1, NEG, ze)
    v2 = jnp.max(ze2, axis=0, keepdims=True)
    i2 = first_argmax(ze2, v2)
    t = jnp.exp(v2 - v1)
    w1 = g_w / (1.0 + t)
    w2 = w1 * t
    hot1 = row == i1
    hot2 = row == i2
    one_hot = jnp.where(hot1, 1.0, jnp.where(hot2, 1.0, 0.0))
    before = prefix_of(one_hot)
    rank1 = jnp.sum(jnp.where(hot1, before, 0.0), axis=0, keepdims=True)
    rank2 = jnp.sum(jnp.where(hot2, before, 0.0), axis=0, keepdims=True)
    fields = ((R_E1, (i1 - ROUTE_OFF).astype(F32)), (R_E2, (i2 - ROUTE_OFF).astype(F32)), (R_W1, w1), (R_W2, w2),
              (R_RANK1, rank1), (R_RANK2, rank2))
    frow = lax.broadcasted_iota(jnp.int32, (R_FIELDS, zt.shape[1]), 0)
    rec = jnp.zeros((R_FIELDS, zt.shape[1]), F32)
    for pos, val in fields:
        rec = jnp.where(frow == pos, val, rec)
    return rec, one_hot


def _outproj_kernel(ao_ref, go_ref, x_ref, woa_ref, wog_ref, g2_ref, wr_ref, br_ref,
                    y1_ref, h2_ref, rect_ref, cnt_ref, cnt_s):
    tm = x_ref.shape[0]

    @pl.when(pl.program_id(0) == 0)
    def _():
        cnt_s[...] = jnp.zeros_like(cnt_s)

    y1 = x_ref[...] + _dot(ao_ref[...], woa_ref[...]) + _dot(go_ref[...], wog_ref[...])
    y1_ref[...] = y1
    ms = jnp.mean(y1 * y1, axis=-1, keepdims=True)
    h2 = (y1 * lax.rsqrt(ms + EPS)) * g2_ref[...]
    hh = h2.astype(BF16)
    hb = hh.astype(F32)
    h2_ref[...] = hb
    hl = (h2 - hb).astype(BF16)
    zz = _dot(hh, wr_ref[...])
    z = zz[:, :LANES] + zz[:, LANES:] + _dot(hl, wr_ref[:, :LANES]) + br_ref[...]
    zt = z.T[:ROUTE_ROWS, :]

    rr = lax.broadcasted_iota(jnp.int32, (tm, tm), 0)
    cc = lax.broadcasted_iota(jnp.int32, (tm, tm), 1)
    earlier = jnp.where(rr < cc, 1.0, 0.0).astype(BF16)

    def prefix_of(one_hot):
        return _dot(one_hot.astype(BF16), earlier) + cnt_s[...]

    rec, one_hot = _route(zt, prefix_of)
    rect_ref[...] = rec
    cnt_s[...] += jnp.sum(one_hot, axis=1, keepdims=True)
    cnt_ref[...] = cnt_s[...]


def _outproj(ao, go, x, woa, wog, g2, wr, br):
    n = x.shape[0]
    tm = min(512, n)
    row = lambda w: pl.BlockSpec((tm, w), lambda i: (i, 0))
    full = lambda a: pl.BlockSpec(a.shape, lambda i: (0, 0))
    return pl.pallas_call(
        _outproj_kernel,
        grid=(n // tm,),
        in_specs=[row(A_WIDTH), row(G_WIDTH), row(D_MODEL), full(woa), full(wog), full(g2), full(wr), full(br)],
        out_specs=[row(D_MODEL), row(D_MODEL),
                   pl.BlockSpec((R_FIELDS, tm), lambda i: (0, i)), pl.BlockSpec((ROUTE_ROWS, 1), lambda i: (0, 0))],
        out_shape=[jax.ShapeDtypeStruct((n, D_MODEL), F32), jax.ShapeDtypeStruct((n, D_MODEL), F32),
                   jax.ShapeDtypeStruct((R_FIELDS, n), F32), jax.ShapeDtypeStruct((ROUTE_ROWS, 1), F32)],
        scratch_shapes=[pltpu.VMEM((ROUTE_ROWS, 1), F32)],
        compiler_params=_cparams(("arbitrary",)),
        name="outproj_router",
    )(ao, go, x, woa, wog, g2, wr, br)


def _row(ref, r):
    return ref.at[pl.ds(r, 1), :]


def _dispatch_kernel(pos1_ref, pos2_ref, h_ref, xs_ref, sem):
    tm = h_ref.shape[0]

    def copies(t):
        return [pltpu.make_async_copy(_row(h_ref, t), _row(xs_ref, p[0, t]), sem) for p in (pos1_ref, pos2_ref)]

    def start(t, c):
        for cp in copies(t):
            cp.start()
        return c

    def wait(t, c):
        for cp in copies(t):
            cp.wait()
        return c

    lax.fori_loop(0, tm, start, 0, unroll=MOE_DMA_UNROLL)
    lax.fori_loop(0, tm, wait, 0, unroll=MOE_DMA_UNROLL)


def _dispatch(pos1, pos2, h2, *, tm):
    n = h2.shape[0]
    idx = pl.BlockSpec((None, 1, tm), lambda i: (i, 0, 0), memory_space=pltpu.SMEM)
    return pl.pallas_call(
        _dispatch_kernel,
        grid=(n // tm,),
        in_specs=[idx, idx, pl.BlockSpec((tm, D_MODEL), lambda i: (i, 0))],
        out_specs=pl.BlockSpec(memory_space=pl.ANY),
        out_shape=jax.ShapeDtypeStruct((2 * n, D_MODEL), F32),
        scratch_shapes=[pltpu.SemaphoreType.DMA(())],
        compiler_params=_cparams(("arbitrary",)),
        name="moe_dispatch",
    )(pos1, pos2, h2)


def _experts_kernel(vt_ref, ve_ref, vlo_ref, vhi_ref, vfirst_ref, xs_ref, wg_ref, wu_ref, wd_ref, ys_ref,
                    wg_s, wu_s, wd_s):
    v = pl.program_id(0)
    t = xs_ref.shape[0]
    rg = min(MOE_ROW_GROUP, t)
    groups = range(t // rg)

    @pl.when(jnp.logical_or(v == 0, ve_ref[v] != ve_ref[jnp.maximum(v - 1, 0)]))
    def _():
        wg_s[...] = wg_ref[...].astype(BF16)
        wu_s[...] = wu_ref[...].astype(BF16)
        wd_s[...] = wd_ref[...].astype(BF16)

    wg = wg_s[...]
    wu = wu_s[...]
    wd = wd_s[...]
    outs = []
    for g in groups:
        x = xs_ref[g * rg:(g + 1) * rg, :].astype(BF16)
        gate = _dot(x, wg)
        a = ((gate * jax.nn.sigmoid(gate)) * _dot(x, wu)).astype(BF16)
        row = lax.broadcasted_iota(jnp.int32, (rg, D_MODEL), 0) + g * rg
        outs.append((_dot(a, wd), jnp.logical_and(row >= vlo_ref[v], row < vhi_ref[v])))

    @pl.when(vfirst_ref[v] == 1)
    def _():
        for g in groups:
            d, mine = outs[g]
            ys_ref[g * rg:(g + 1) * rg, :] = jnp.where(mine, d, 0.0)

    @pl.when(vfirst_ref[v] == 0)
    def _():
        for g in groups:
            d, mine = outs[g]
            ys_ref[g * rg:(g + 1) * rg, :] = jnp.where(mine, d, ys_ref[g * rg:(g + 1) * rg, :])


def _experts(visits, xs, w_gate, w_up, w_down, *, t):
    nv = visits[0].shape[0]
    slot = pl.BlockSpec((t, D_MODEL), lambda v, vt, ve, lo, hi, fi: (vt[v], 0))
    w_in = pl.BlockSpec((None, D_MODEL, D_EXPERT), lambda v, vt, ve, lo, hi, fi: (ve[v], 0, 0))
    w_out = pl.BlockSpec((None, D_EXPERT, D_MODEL), lambda v, vt, ve, lo, hi, fi: (ve[v], 0, 0))
    return pl.pallas_call(
        _experts_kernel,
        grid_spec=pltpu.PrefetchScalarGridSpec(
            num_scalar_prefetch=5, grid=(nv,), in_specs=[slot, w_in, w_in, w_out], out_specs=slot,
            scratch_shapes=[pltpu.VMEM((D_MODEL, D_EXPERT), BF16), pltpu.VMEM((D_MODEL, D_EXPERT), BF16),
                            pltpu.VMEM((D_EXPERT, D_MODEL), BF16)]),
        out_shape=jax.ShapeDtypeStruct(xs.shape, F32),
        compiler_params=_cparams(("arbitrary",)),
        name="moe_experts",
    )(*visits, xs, w_gate, w_up, w_down)


def _combine_kernel(pos1_ref, pos2_ref, y1_ref, rect_ref, ys_ref, out_ref, g1_s, g2_s, sem):
    tm = y1_ref.shape[0]

    def copies(t):
        return [pltpu.make_async_copy(_row(ys_ref, p[0, t]), _row(g, t), sem)
                for p, g in ((pos1_ref, g1_s), (pos2_ref, g2_s))]

    def start(t, c):
        for cp in copies(t):
            cp.start()
        return c

    def wait(t, c):
        for cp in copies(t):
            cp.wait()
        return c

    lax.fori_loop(0, tm, start, 0, unroll=MOE_DMA_UNROLL)
    rec = rect_ref[...].T
    w1 = rec[:, R_W1:R_W1 + 1]
    w2 = rec[:, R_W2:R_W2 + 1]
    lax.fori_loop(0, tm, wait, 0, unroll=MOE_DMA_UNROLL)
    out_ref[...] = y1_ref[...] + w1 * g1_s[...] + w2 * g2_s[...]


def _combine(pos1, pos2, y1, rect, ys, *, tm):
    n = y1.shape[0]
    idx = pl.BlockSpec((None, 1, tm), lambda i: (i, 0, 0), memory_space=pltpu.SMEM)
    row = lambda w: pl.BlockSpec((tm, w), lambda i: (i, 0))
    return pl.pallas_call(
        _combine_kernel,
        grid=(n // tm,),
        in_specs=[idx, idx, row(D_MODEL), pl.BlockSpec((R_FIELDS, tm), lambda i: (0, i)),
                  pl.BlockSpec(memory_space=pl.ANY)],
        out_specs=row(D_MODEL),
        out_shape=jax.ShapeDtypeStruct((n, D_MODEL), F32),
        scratch_shapes=[pltpu.VMEM((tm, D_MODEL), F32), pltpu.VMEM((tm, D_MODEL), F32),
                        pltpu.SemaphoreType.DMA(())],
        compiler_params=_cparams(("arbitrary",)),
        name="moe_combine",
    )(pos1, pos2, y1, rect, ys)


def _expert_visits(counts, n_slots, t):
    n_tiles = n_slots // t
    nv = n_tiles + N_EXPERTS - 1
    end = jnp.cumsum(counts)
    start = end - counts
    first_tile = start // t
    last_tile = jnp.maximum(end - 1, start) // t
    n_vis = jnp.where(counts > 0, last_tile - first_tile + 1, 0)
    vis_end = jnp.cumsum(n_vis)
    total = vis_end[-1]
    v = jnp.arange(nv, dtype=jnp.int32)
    e = jnp.minimum(jnp.sum((vis_end[None, :] <= v[:, None]).astype(jnp.int32), axis=1), N_EXPERTS - 1)
    pick = lambda a: jnp.sum(jnp.where(e[:, None] == jnp.arange(N_EXPERTS)[None, :], a[None, :], 0), axis=1)
    tile = pick(first_tile) + (v - (pick(vis_end) - pick(n_vis)))
    lo = jnp.maximum(pick(start), tile * t) - tile * t
    hi = jnp.minimum(pick(end), (tile + 1) * t) - tile * t
    live = v < total
    last_e = jnp.max(jnp.where(counts > 0, jnp.arange(N_EXPERTS), 0))
    tile = jnp.where(live, tile, n_tiles - 1)
    e = jnp.where(live, e, last_e)
    lo = jnp.where(live, lo, 0)
    hi = jnp.where(live, hi, 0)
    first = jnp.concatenate([jnp.ones((1,), jnp.int32), (tile[1:] != tile[:-1]).astype(jnp.int32)])
    return tuple(a.astype(jnp.int32) for a in (tile, e, lo, hi, first))


def _moe(h2, y1, rect, cnt, w_gate, w_up, w_down):
    n = y1.shape[0]
    tm = min(MOE_TM, n)
    t = min(MOE_SLOT_TILE, 2 * n)
    counts = cnt[ROUTE_OFF:ROUTE_OFF + N_EXPERTS, 0].astype(jnp.int32)
    start = jnp.cumsum(counts) - counts
    ids = rect.astype(jnp.int32)
    experts = jnp.arange(N_EXPERTS, dtype=jnp.int32)[:, None]

    def slot(e_row, rank_row):
        return (jnp.sum(jnp.where(ids[e_row][None, :] == experts, start[:, None], 0), axis=0)
                + ids[rank_row]).reshape(n // tm, 1, tm)

    pos1 = slot(R_E1, R_RANK1)
    pos2 = slot(R_E2, R_RANK2)
    xs = _dispatch(pos1, pos2, h2, tm=tm)
    ys = _experts(_expert_visits(counts, 2 * n, t), xs, w_gate, w_up, w_down, t=t)
    return _combine(pos1, pos2, y1, rect, ys, tm=tm)


def _layer_weights(l, norm1_gain, w_in, a_q_gain, a_k_gain, lambda_q1, lambda_k1, lambda_q2, lambda_k2,
                   a_out_gain, w_a2, b_a, g_out_gain, w_out, norm2_gain, w_group, b_group, w_erouter,
                   b_erouter, w_gate, w_up, w_down):
    w = w_in[l]
    w_la = _fold(w[:, _C_GA:_C_GR], w_a2[l])
    qk_cols = []
    for h in range(G_HEADS):
        qk_cols += [w[:, _C_GQ + h * G_DK:_C_GQ + (h + 1) * G_DK], w[:, _C_GK + h * G_DK:_C_GK + (h + 1) * G_DK]]
    w_big = jnp.concatenate([w[:, :_C_GQ]] + qk_cols + [w[:, _C_GV:_C_GA], w[:, _C_GR:], w_la], axis=1).astype(BF16)
    w_r = jnp.concatenate([w_group[l], w_erouter[l],
                           jnp.zeros((D_MODEL, LANES - N_GROUPS - N_EXPERTS), F32)], axis=1)
    w_rh = w_r.astype(BF16)
    w_r2 = jnp.concatenate([w_rh, (w_r - w_rh.astype(F32)).astype(BF16)], axis=1)
    b_r = jnp.concatenate([b_group[l], b_erouter[l], jnp.zeros((LANES - N_GROUPS - N_EXPERTS,), F32)])[None, :]
    return dict(
        g1=norm1_gain[l][None, :], w_big=w_big,
        qg2=jnp.tile(a_q_gain[l], 2)[None, :], kg2=jnp.tile(a_k_gain[l], 2)[None, :],
        ba=b_a[l][None, :],
        lams=(lambda_q1[l][None, :], lambda_k1[l][None, :], lambda_q2[l][None, :], lambda_k2[l][None, :]),
        a_gain=a_out_gain[l][None, :], g_gain=g_out_gain[l][None, :],
        woa=w_out[l][:A_WIDTH].astype(BF16), wog=w_out[l][A_WIDTH:].astype(BF16),
        g2=norm2_gain[l][None, :], w_r2=w_r2, b_r=b_r,
        w_gate=w_gate[l], w_up=w_up[l], w_down=w_down[l],
    )


def _mix_and_moe(x, p, *, batch, seq, chunk, lam_init, cache=None, state=None):
    prompt = cache is None
    res = _inproj(x, p["g1"], p["w_big"], p["qg2"], p["kg2"], p["ba"], batch=batch, seq=seq, final_layout=prompt)
    if prompt:
        q, kt, ktb, v4, vb, qk, gv, gr, la = res
        ao = _attn_prompt(p["lams"], p["a_gain"], q, ktb, vb, batch=batch, seq=seq, lam_init=lam_init)
        s0 = jnp.zeros((batch, G_HEADS, G_DK, G_DV), F32)
        k_out = jnp.transpose(kt.reshape(batch, A_HEADS, 2, A_DH, seq), (0, 4, 1, 2, 3))
        v_out = v4.reshape(batch, seq, A_HEADS, A_DV)
    else:
        q, k, v, qk, gv, gr, la = res
        kct, vc = cache
        ao = _attn_sample(p["lams"], p["a_gain"], q, k, v, kct, vc, batch=batch, seq=seq,
                          past=kct.shape[-1], lam_init=lam_init)
        s0 = state
        k_out = k.reshape(batch, seq, A_HEADS, 2, A_DH)
        v_out = v.reshape(batch, seq, A_HEADS, A_DV)
    go, st = _gla(qk, la, gv, gr, p["g_gain"], s0, batch=batch, seq=seq, chunk=chunk)
    y1, h2, rect, cnt = _outproj(ao, go, x, p["woa"], p["wog"], p["g2"], p["w_r2"], p["b_r"])
    y = _moe(h2, y1, rect, cnt, p["w_gate"], p["w_up"], p["w_down"])
    return y, k_out, v_out, st


def kernel(x_prompt, x_sample, cache_k, cache_v, state_gla, norm1_gain, w_in, a_q_gain, a_k_gain, lambda_q1, lambda_k1, lambda_q2, lambda_k2, a_out_gain, w_a2, b_a, g_out_gain, w_out, norm2_gain, w_group, b_group, w_erouter, b_erouter, w_gate, w_up, w_down):
    weights = (norm1_gain, w_in, a_q_gain, a_k_gain, lambda_q1, lambda_k1, lambda_q2, lambda_k2, a_out_gain,
               w_a2, b_a, g_out_gain, w_out, norm2_gain, w_group, b_group, w_erouter, b_erouter, w_gate,
               w_up, w_down)
    depth = w_in.shape[0]
    pb, pl_, d = x_prompt.shape
    sb, sl, _ = x_sample.shape
    past = cache_k.shape[2]
    y_p = x_prompt.reshape(pb * pl_, d)
    y_s = x_sample.reshape(sb * sl, d)
    outs = [[] for _ in range(6)]
    for l in range(depth):
        lam_init = 0.8 - 0.6 * math.exp(-0.3 * l)
        p = _layer_weights(l, *weights)
        y_p, kp, vp, sp = _mix_and_moe(y_p, p, batch=pb, seq=pl_, chunk=CHUNK, lam_init=lam_init)
        cache = (jnp.transpose(cache_k[l], (0, 2, 3, 4, 1)).reshape(sb, A_WIDTH, past),
                 cache_v[l].reshape(sb * past * A_HEADS, A_DV))
        y_s, kn, vn, sn = _mix_and_moe(y_s, p, batch=sb, seq=sl, chunk=sl, lam_init=lam_init,
                                       cache=cache, state=state_gla[l])
        for o, t in zip(outs, (kp, vp, sp, kn, vn, sn)):
            o.append(t)
    return (y_p.reshape(pb, pl_, d), y_s.reshape(sb, sl, d)) + tuple(jnp.stack(o) for o in outs)
```

```python
import functools
import math

import jax
import jax.numpy as jnp
from jax import lax
from jax.experimental import pallas as pl
from jax.experimental.pallas import tpu as pltpu

F32 = jnp.float32
BF16 = jnp.bfloat16

D_MODEL = 1024
CHUNK = 64
A_HEADS = 4
A_DH = 64
A_DV = 128
A_WIDTH = A_HEADS * A_DV
G_HEADS = 4
G_DK = 64
G_DV = 128
G_WIDTH = G_HEADS * G_DV
G_RANK = 16
G_TAU = 16.0
N_GROUPS = 4
EXPERTS_PER_GROUP = 8
N_EXPERTS = N_GROUPS * EXPERTS_PER_GROUP
D_EXPERT = D_MODEL // 4
EPS = 1e-6

LANES = 128
NEG = -1e30
VMEM_LIMIT = 56 * 1024 * 1024

_C_AQ, _C_AK, _C_AV = 0, 512, 1024
_C_GQ, _C_GK, _C_GV = 1536, 1792, 2048
_C_GA, _C_GR = 2560, 2576
W_BIG = 3328

INPROJ_TM = 512
ATT_TQ = 1024
ATT_TK = 1024
ATT_RG = 256
LOG2E = 1.4426950408889634
ATT_SAFE_BOUND = 43.0 * LOG2E
GLA_UNROLL = 8
ROUTE_OFF = N_GROUPS
MOE_TM = 1024
MOE_DMA_UNROLL = 8
MOE_SLOT_TILE = 512
MOE_ROW_GROUP = 256


def _cparams(sem):
    return pltpu.CompilerParams(dimension_semantics=sem, vmem_limit_bytes=VMEM_LIMIT)


def _nt(a, b):
    return lax.dot_general(a, b, (((1,), (1,)), ((), ())), preferred_element_type=F32)


def _tn(a, b):
    return lax.dot_general(a, b, (((0,), (0,)), ((), ())), preferred_element_type=F32)


def _dot(a, b):
    return jnp.dot(a, b, preferred_element_type=F32)


def _fold_kernel(wga_ref, wa2_ref, out_ref):
    out_ref[...] = jnp.dot(wga_ref[...], wa2_ref[...], preferred_element_type=F32,
                           precision=lax.Precision.HIGHEST)


def _fold(w_ga, w_a2):
    return pl.pallas_call(
        _fold_kernel,
        out_shape=jax.ShapeDtypeStruct((D_MODEL, G_HEADS * G_DK), F32),
        name="fold_gate",
    )(w_ga, w_a2)


def _headnorm(z, gain2):
    outs = []
    lane = lax.broadcasted_iota(jnp.int32, (z.shape[0], LANES), 1)
    lo = lane < A_DH
    for c in range(z.shape[1] // LANES):
        x = z[:, c * LANES:(c + 1) * LANES]
        xx = x * x
        s_lo = jnp.sum(jnp.where(lo, xx, 0.0), axis=-1, keepdims=True)
        s_hi = jnp.sum(jnp.where(lo, 0.0, xx), axis=-1, keepdims=True)
        r = jnp.where(lo, lax.rsqrt(s_lo * (1.0 / A_DH) + EPS), lax.rsqrt(s_hi * (1.0 / A_DH) + EPS))
        outs.append((x * r) * gain2)
    return outs


def _inproj_kernel(x_ref, g1_ref, w_ref, qg_ref, kg_ref, ba_ref, *out_refs, final_layout):
    if final_layout:
        q_ref, kt_ref, ktb_ref, v4_ref, vb_ref, qk_ref, gv_ref, gr_ref, la_ref = out_refs
    else:
        q_ref, k_ref, v_ref, qk_ref, gv_ref, gr_ref, la_ref = out_refs
    x = x_ref[...]
    tm = x.shape[0]
    ms = jnp.mean(x * x, axis=-1, keepdims=True)
    h = ((x * lax.rsqrt(ms + EPS)) * g1_ref[...]).astype(BF16)

    def seg(lo, hi):
        return _dot(h, w_ref[:, lo:hi])

    for c, y in enumerate(_headnorm(seg(0, 512), qg_ref[...])):
        q_ref[:, c * LANES:(c + 1) * LANES] = (y * (A_DH ** -0.5 * LOG2E)).astype(BF16)
    for c, y in enumerate(_headnorm(seg(512, 1024), kg_ref[...])):
        cols = slice(c * LANES, (c + 1) * LANES)
        if final_layout:
            yt = y.T
            kt_ref[cols, :] = yt
            ktb_ref[cols, :] = yt.astype(BF16)
        else:
            k_ref[:, cols] = y
    v = seg(1024, 1536)
    if final_layout:
        vb_ref[...] = v.astype(BF16)
        for c in range(A_HEADS):
            v4_ref[pl.ds(c, tm, stride=A_HEADS), :] = v[:, c * LANES:(c + 1) * LANES]
    else:
        v_ref[...] = v
    qk_ref[...] = seg(1536, 2048)
    gv_ref[...] = seg(2048, 2560)
    gr_ref[...] = seg(2560, 3072)
    zl = seg(3072, 3328) + ba_ref[...]
    la_ref[...] = (jnp.minimum(zl, 0.0) - jnp.log1p(jnp.exp(-jnp.abs(zl)))) * (1.0 / G_TAU)


def _inproj(x, g1, w_big, qg2, kg2, ba, *, batch, seq, final_layout):
    n = x.shape[0]
    tm = min(INPROJ_TM, seq)
    nlt = seq // tm
    row = lambda w: pl.BlockSpec((tm, w), lambda i: (i, 0))
    full = lambda a: pl.BlockSpec(a.shape, lambda i: (0, 0))
    tail = [(row(512), (n, 512), F32)] * 3 + [(row(256), (n, 256), F32)]
    if final_layout:
        outs = [
            (row(512), (n, 512), BF16),
            (pl.BlockSpec((None, 512, tm), lambda i: (i // nlt, 0, i % nlt)), (batch, 512, seq), F32),
            (pl.BlockSpec((None, None, 512, tm), lambda i: (i // nlt, i % nlt, 0, 0)),
             (batch, nlt, 512, tm), BF16),
            (pl.BlockSpec((tm * A_HEADS, LANES), lambda i: (i, 0)), (n * A_HEADS, LANES), F32),
            (row(512), (n, 512), BF16),
        ] + tail
    else:
        outs = [(row(512), (n, 512), BF16), (row(512), (n, 512), F32), (row(512), (n, 512), F32)] + tail
    return pl.pallas_call(
        functools.partial(_inproj_kernel, final_layout=final_layout),
        grid=(n // tm,),
        in_specs=[row(D_MODEL), full(g1), full(w_big), full(qg2), full(kg2), full(ba)],
        out_specs=[o[0] for o in outs],
        out_shape=[jax.ShapeDtypeStruct(o[1], o[2]) for o in outs],
        compiler_params=_cparams(("parallel",)),
        name="inproj",
    )(x, g1, w_big, qg2, kg2, ba)


def _diff_lambda(lq1_ref, lk1_ref, lq2_ref, lk2_ref, lam_init):
    a = jnp.sum(lq1_ref[...] * lk1_ref[...], axis=-1, keepdims=True)
    b = jnp.sum(lq2_ref[...] * lk2_ref[...], axis=-1, keepdims=True)
    return jnp.exp(a) - jnp.exp(b) + lam_init


def _attn_prompt_kernel(lq1_ref, lk1_ref, lq2_ref, lk2_ref, gain_ref, q_ref, kt_ref, v_ref,
                        o_ref, acc_s, *, seq, lam_init):
    tq, tk, rg = ATT_TQ, ATT_TK, ATT_RG
    kb = kt_ref.shape[-1]
    parts = tq // rg
    nrg = 2 * parts
    lam = _diff_lambda(lq1_ref, lk1_ref, lq2_ref, lk2_ref, lam_init)
    lane = lax.broadcasted_iota(jnp.int32, (tq, LANES), 1)
    lo = lane < A_DH
    ones = jnp.ones((tk, LANES), BF16)
    unit = jnp.where(lax.broadcasted_iota(jnp.int32, (LANES, tk), 0) == 0, 1.0, 0.0).astype(BF16)

    def visible(part, nk):
        qq = lax.broadcasted_iota(jnp.int32, (rg, nk), 0) + part * rg
        kk = lax.broadcasted_iota(jnp.int32, (rg, nk), 1)
        return (kk // CHUNK) <= (qq // CHUNK)

    def key_norms(j, c):
        k = kt_ref[j].astype(F32)
        sq = k * k
        n1 = jnp.max(jnp.sum(sq[:A_DH], axis=0, keepdims=True), axis=1, keepdims=True)
        n2 = jnp.max(jnp.sum(sq[A_DH:], axis=0, keepdims=True), axis=1, keepdims=True)
        return jnp.maximum(c[0], n1), jnp.maximum(c[1], n2)

    zero11 = jnp.zeros((1, 1), F32)
    k1sq, k2sq = lax.fori_loop(0, seq // kb, key_norms, (zero11, zero11))

    def keys(j):
        return jnp.concatenate([kt_ref[j * (tk // kb) + b] for b in range(tk // kb)], axis=1)

    def values(j):
        vb = v_ref[pl.ds(pl.multiple_of(j * tk, tk), tk), :]
        return jnp.concatenate([vb, ones], axis=1)

    def shifted_step(j, carry, qx, diagonal):
        ktx = jnp.concatenate([keys(j), unit], axis=0)
        vx = values(j)
        for g in range(nrg):
            rows = slice(g * rg, (g + 1) * rg)
            part = g % parts
            nk = (part + 1) * rg if diagonal else tk
            p = jnp.exp2(_dot(qx[rows], ktx[:, :nk]))
            if diagonal:
                p = jnp.where(visible(part, nk), p, 0.0)
            acc_s[rows, :] += _dot(p.astype(BF16), vx[:nk])
        return carry

    def running_max_step(j, ms, qz, diagonal):
        kt = keys(j)
        vx = values(j)
        out = []
        for g in range(nrg):
            rows = slice(g * rg, (g + 1) * rg)
            s = _dot(qz[rows], kt)
            if diagonal:
                s = jnp.where(visible(g % parts, tk), s, NEG)
            m_new = jnp.maximum(ms[g], jnp.max(s, axis=-1, keepdims=True))
            alpha = jnp.exp2(ms[g] - m_new)
            p = jnp.exp2(s - m_new).astype(BF16)
            acc_s[rows, :] = alpha * acc_s[rows, :] + _dot(p, vx)
            out.append(m_new)
        return tuple(out)

    def q_block(i, carry):
        qrows = pl.ds(pl.multiple_of(i * tq, tq), tq)
        qi = q_ref[qrows, :]
        zero = jnp.zeros_like(qi)
        q1 = jnp.where(lo, qi, zero)
        q2 = jnp.where(lo, zero, qi)
        qf = qi.astype(F32)
        sq = qf * qf
        shift1 = jnp.sqrt(jnp.sum(jnp.where(lo, sq, 0.0), axis=-1, keepdims=True) * k1sq)
        shift2 = jnp.sqrt(jnp.sum(jnp.where(lo, 0.0, sq), axis=-1, keepdims=True) * k2sq)
        safe = jnp.max(jnp.maximum(shift1, shift2)) <= ATT_SAFE_BOUND
        acc_s[...] = jnp.zeros_like(acc_s)

        @pl.when(safe)
        def _():
            x1 = jnp.where(lane == 0, -shift1, 0.0).astype(BF16)
            x2 = jnp.where(lane == 0, -shift2, 0.0).astype(BF16)
            qx = jnp.concatenate([jnp.concatenate([q1, x1], axis=1), jnp.concatenate([q2, x2], axis=1)], axis=0)
            lax.fori_loop(0, i, lambda j, c: shifted_step(j, c, qx, False), 0)
            shifted_step(i, 0, qx, True)

        @pl.when(jnp.logical_not(safe))
        def _():
            qz = jnp.concatenate([q1, q2], axis=0)
            ms = (jnp.full((rg, 1), NEG, F32),) * nrg
            ms = lax.fori_loop(0, i, lambda j, c: running_max_step(j, c, qz, False), ms)
            running_max_step(i, ms, qz, True)

        a1 = acc_s[:tq, :]
        a2 = acc_s[tq:, :]
        o = a1[:, :A_DV] / a1[:, A_DV:] - lam * (a2[:, :A_DV] / a2[:, A_DV:])
        msq = jnp.mean(o * o, axis=-1, keepdims=True)
        o_ref[qrows, :] = (((o * lax.rsqrt(msq + EPS)) * gain_ref[...]) * (1.0 - lam_init)).astype(o_ref.dtype)
        return carry

    lax.fori_loop(0, seq // tq, q_block, 0)


def _attn_prompt(lams, gain_row, q, ktb, v, *, batch, seq, lam_init):
    nkb, kb = ktb.shape[1], ktb.shape[3]
    assert seq % ATT_TQ == 0 and ATT_TQ == ATT_TK and ATT_TK % kb == 0, (seq, kb)
    vec = pl.BlockSpec((1, A_DH), lambda b, h: (0, 0))
    head = pl.BlockSpec((seq, LANES), lambda b, h: (b, h))
    return pl.pallas_call(
        functools.partial(_attn_prompt_kernel, seq=seq, lam_init=lam_init),
        grid=(batch, A_HEADS),
        in_specs=[vec, vec, vec, vec, pl.BlockSpec((1, A_DV), lambda b, h: (0, 0)), head,
                  pl.BlockSpec((None, nkb, LANES, kb), lambda b, h: (b, 0, h, 0)), head],
        out_specs=head,
        out_shape=jax.ShapeDtypeStruct((batch * seq, A_WIDTH), BF16),
        scratch_shapes=[pltpu.VMEM((2 * ATT_TQ, A_DV + LANES), F32)],
        compiler_params=_cparams(("parallel", "parallel")),
        name="attn_prompt",
    )(*lams, gain_row, q, ktb, v)


def _attn_sample_kernel(lq1_ref, lk1_ref, lq2_ref, lk2_ref, gain_ref, q_ref, kn_ref, vn_ref,
                        kct_ref, vc_ref, o_ref, *, past, lam_init):
    lam = _diff_lambda(lq1_ref, lk1_ref, lq2_ref, lk2_ref, lam_init)
    nq = q_ref.shape[0]
    lane = lax.broadcasted_iota(jnp.int32, (nq, LANES), 1)
    for h in range(A_HEADS):
        cols = slice(h * LANES, (h + 1) * LANES)
        q = q_ref[:, cols]
        zero = jnp.zeros_like(q)
        qz = jnp.concatenate([jnp.where(lane < A_DH, q, zero), jnp.where(lane < A_DH, zero, q)], axis=0)
        vc = vc_ref[pl.ds(h, past, stride=A_HEADS), :].astype(BF16)
        sc = _dot(qz, kct_ref[cols, :].astype(BF16))
        sn = _nt(qz, kn_ref[:, cols].astype(BF16))
        m = jnp.maximum(jnp.max(sc, axis=-1, keepdims=True), jnp.max(sn, axis=-1, keepdims=True))
        ec = jnp.exp2(sc - m)
        en = jnp.exp2(sn - m)
        l = jnp.sum(ec, axis=-1, keepdims=True) + jnp.sum(en, axis=-1, keepdims=True)
        pv = _dot(ec.astype(BF16), vc) + _dot(en.astype(BF16), vn_ref[:, cols].astype(BF16))
        pv = pv * (1.0 / l)
        o = pv[:nq] - lam * pv[nq:]
        ms = jnp.mean(o * o, axis=-1, keepdims=True)
        o_ref[:, cols] = (((o * lax.rsqrt(ms + EPS)) * gain_ref[...]) * (1.0 - lam_init)).astype(o_ref.dtype)


def _attn_sample(lams, gain_row, q, k, v, kct, vc, *, batch, seq, past, lam_init):
    vec = pl.BlockSpec((1, A_DH), lambda b: (0, 0))
    new = pl.BlockSpec((seq, A_WIDTH), lambda b: (b, 0))
    return pl.pallas_call(
        functools.partial(_attn_sample_kernel, past=past, lam_init=lam_init),
        grid=(batch,),
        in_specs=[vec, vec, vec, vec, pl.BlockSpec((1, A_DV), lambda b: (0, 0)), new, new, new,
                  pl.BlockSpec((None, A_WIDTH, past), lambda b: (b, 0, 0)),
                  pl.BlockSpec((past * A_HEADS, LANES), lambda b: (b, 0))],
        out_specs=new,
        out_shape=jax.ShapeDtypeStruct((batch * seq, A_WIDTH), BF16),
        compiler_params=_cparams(("parallel",)),
        name="attn_sample",
    )(*lams, gain_row, q, k, v, kct, vc)


def _cumsum_rows(x):
    n = x.shape[0]
    row = lax.broadcasted_iota(jnp.int32, x.shape, 0)
    s = 1
    while s < n:
        x = x + jnp.where(row >= s, pltpu.roll(x, s, 0), 0.0)
        s *= 2
    return x


def _gla_kernel(qk_ref, la_ref, v_ref, gr_ref, gain_ref, s0_ref, go_ref, st_ref, st_s, *, chunk, rows_per_step):
    c = chunk
    n_chunks = rows_per_step // c
    unroll = GLA_UNROLL if n_chunks % GLA_UNROLL == 0 else 1

    @pl.when(pl.program_id(1) == 0)
    def _():
        st_s[...] = s0_ref[...]

    lane = lax.broadcasted_iota(jnp.int32, (c, LANES), 1)
    lo = lane < G_DK
    causal = lax.broadcasted_iota(jnp.int32, (c, c), 0) >= lax.broadcasted_iota(jnp.int32, (c, c), 1)
    diag = lax.broadcasted_iota(jnp.int32, (G_DK, LANES), 0) == lax.broadcasted_iota(jnp.int32, (G_DK, LANES), 1)
    scale = G_DK ** -0.5
    pad_lanes = jnp.zeros((c, LANES - c), F32)
    pad_rows = [jnp.zeros((G_DK - c, G_DV), BF16)] if c < G_DK else []

    def one_chunk(ci):
        rows = pl.ds(pl.multiple_of(ci * c, c), c)
        bsum = _cumsum_rows(la_ref[rows, :])
        for h in range(G_HEADS):
            cols = slice(h * LANES, (h + 1) * LANES)
            qk = qk_ref[rows, cols]
            kq = pltpu.roll(qk, G_DK, 1)
            bg = bsum[:, (h // 2) * LANES:(h // 2 + 1) * LANES]
            br = pltpu.roll(bg, G_DK, 1)
            b = jnp.where(lo, bg, br) if h % 2 == 0 else jnp.where(lo, br, bg)
            b_last = b[c - 1:c, :]
            b_mid = b[c // 2 - 1:c // 2, :]
            qt = (qk * jnp.exp(b - b_mid)) * scale
            kt = kq * jnp.exp(b_mid - b)
            qe = (kq * jnp.exp(b)) * scale
            kh = kq * jnp.exp(b_last - b)
            sc = jnp.where(causal, _nt(qt[:, :G_DK].astype(BF16), kt[:, :G_DK].astype(BF16)), 0.0)
            lhs = jnp.where(lo, jnp.concatenate([sc, pad_lanes], axis=1), qe).astype(BF16)
            vb = v_ref[rows, cols].astype(BF16)
            st = st_s[h]
            o = _dot(lhs, jnp.concatenate([vb] + pad_rows + [st.astype(BF16)], axis=0))
            decay = jnp.sum(jnp.where(diag, jnp.exp(b_last), 0.0), axis=-1, keepdims=True)
            st_s[h] = decay * st + _tn(kh[:, :G_DK].astype(BF16), vb)
            ms = jnp.mean(o * o, axis=-1, keepdims=True)
            on = (o * lax.rsqrt(ms + EPS)) * gain_ref[...]
            g = gr_ref[rows, cols]
            go_ref[rows, cols] = (on * (g * jax.nn.sigmoid(g))).astype(go_ref.dtype)

    def chunk_step(i, carry):
        for u in range(unroll):
            one_chunk(i * unroll + u)
        return carry

    lax.fori_loop(0, n_chunks // unroll, chunk_step, 0)
    st_ref[...] = st_s[...]


def _gla(qk, la, v, gr, gain_row, s0, *, batch, seq, chunk):
    lb = min(seq, 1024)
    nl = seq // lb
    wide = pl.BlockSpec((lb, G_WIDTH), lambda b, l: (b * nl + l, 0))
    state = pl.BlockSpec((None, G_HEADS, G_DK, G_DV), lambda b, l: (b, 0, 0, 0))
    return pl.pallas_call(
        functools.partial(_gla_kernel, chunk=chunk, rows_per_step=lb),
        grid=(batch, nl),
        in_specs=[wide, pl.BlockSpec((lb, G_HEADS * G_DK), lambda b, l: (b * nl + l, 0)), wide, wide,
                  pl.BlockSpec((1, G_DV), lambda b, l: (0, 0)), state],
        out_specs=[wide, state],
        out_shape=[jax.ShapeDtypeStruct((batch * seq, G_WIDTH), BF16),
                   jax.ShapeDtypeStruct((batch, G_HEADS, G_DK, G_DV), F32)],
        scratch_shapes=[pltpu.VMEM((G_HEADS, G_DK, G_DV), F32)],
        compiler_params=_cparams(("arbitrary", "arbitrary")),
        name="gla",
    )(qk, la, v, gr, gain_row, s0)


R_E1, R_E2, R_W1, R_W2, R_RANK1, R_RANK2, R_FIELDS = 0, 1, 2, 3, 4, 5, 8
ROUTE_ROWS = 40


def _route(zt, prefix_of):
    row = lax.broadcasted_iota(jnp.int32, zt.shape, 0)
    big = jnp.int32(LANES)

    def first_argmax(vals, vmax):
        return jnp.min(jnp.where(vals == vmax, row, big), axis=0, keepdims=True)

    zg = jnp.where(row < N_GROUPS, zt, NEG)
    gmax = jnp.max(zg, axis=0, keepdims=True)
    g_idx = first_argmax(zg, gmax)
    g_w = 1.0 / jnp.sum(jnp.exp(zg - gmax), axis=0, keepdims=True)
    first = ROUTE_OFF + EXPERTS_PER_GROUP * g_idx
    ze = jnp.where(row < first, NEG, jnp.where(row < first + EXPERTS_PER_GROUP, zt, NEG))
    v1 = jnp.max(ze, axis=0, keepdims=True)
    i1 = first_argmax(ze, v1)
    ze2 = jnp.where(row == i1, NEG, ze)
    v2 = jnp.max(ze2, axis=0, keepdims=True)
    i2 = first_argmax(ze2, v2)
    t = jnp.exp(v2 - v1)
    w1 = g_w / (1.0 + t)
    w2 = w1 * t
    hot1 = row == i1
    hot2 = row == i2
    one_hot = jnp.where(hot1, 1.0, jnp.where(hot2, 1.0, 0.0))
    before = prefix_of(one_hot)
    rank1 = jnp.sum(jnp.where(hot1, before, 0.0), axis=0, keepdims=True)
    rank2 = jnp.sum(jnp.where(hot2, before, 0.0), axis=0, keepdims=True)
    fields = ((R_E1, (i1 - ROUTE_OFF).astype(F32)), (R_E2, (i2 - ROUTE_OFF).astype(F32)), (R_W1, w1), (R_W2, w2),
              (R_RANK1, rank1), (R_RANK2, rank2))
    frow = lax.broadcasted_iota(jnp.int32, (R_FIELDS, zt.shape[1]), 0)
    rec = jnp.zeros((R_FIELDS, zt.shape[1]), F32)
    for pos, val in fields:
        rec = jnp.where(frow == pos, val, rec)
    return rec, one_hot


def _outproj_kernel(ao_ref, go_ref, x_ref, woa_ref, wog_ref, g2_ref, wr_ref, br_ref,
                    y1_ref, h2_ref, rect_ref, cnt_ref, cnt_s):
    tm = x_ref.shape[0]

    @pl.when(pl.program_id(0) == 0)
    def _():
        cnt_s[...] = jnp.zeros_like(cnt_s)

    y1 = x_ref[...] + _dot(ao_ref[...], woa_ref[...]) + _dot(go_ref[...], wog_ref[...])
    y1_ref[...] = y1
    ms = jnp.mean(y1 * y1, axis=-1, keepdims=True)
    h2 = (y1 * lax.rsqrt(ms + EPS)) * g2_ref[...]
    hh = h2.astype(BF16)
    hb = hh.astype(F32)
    h2_ref[...] = hb
    hl = (h2 - hb).astype(BF16)
    zz = _dot(hh, wr_ref[...])
    z = zz[:, :LANES] + zz[:, LANES:] + _dot(hl, wr_ref[:, :LANES]) + br_ref[...]
    zt = z.T[:ROUTE_ROWS, :]

    rr = lax.broadcasted_iota(jnp.int32, (tm, tm), 0)
    cc = lax.broadcasted_iota(jnp.int32, (tm, tm), 1)
    earlier = jnp.where(rr < cc, 1.0, 0.0).astype(BF16)

    def prefix_of(one_hot):
        return _dot(one_hot.astype(BF16), earlier) + cnt_s[...]

    rec, one_hot = _route(zt, prefix_of)
    rect_ref[...] = rec
    cnt_s[...] += jnp.sum(one_hot, axis=1, keepdims=True)
    cnt_ref[...] = cnt_s[...]


def _outproj(ao, go, x, woa, wog, g2, wr, br):
    n = x.shape[0]
    tm = min(512, n)
    row = lambda w: pl.BlockSpec((tm, w), lambda i: (i, 0))
    full = lambda a: pl.BlockSpec(a.shape, lambda i: (0, 0))
    return pl.pallas_call(
        _outproj_kernel,
        grid=(n // tm,),
        in_specs=[row(A_WIDTH), row(G_WIDTH), row(D_MODEL), full(woa), full(wog), full(g2), full(wr), full(br)],
        out_specs=[row(D_MODEL), row(D_MODEL),
                   pl.BlockSpec((R_FIELDS, tm), lambda i: (0, i)), pl.BlockSpec((ROUTE_ROWS, 1), lambda i: (0, 0))],
        out_shape=[jax.ShapeDtypeStruct((n, D_MODEL), F32), jax.ShapeDtypeStruct((n, D_MODEL), F32),
                   jax.ShapeDtypeStruct((R_FIELDS, n), F32), jax.ShapeDtypeStruct((ROUTE_ROWS, 1), F32)],
        scratch_shapes=[pltpu.VMEM((ROUTE_ROWS, 1), F32)],
        compiler_params=_cparams(("arbitrary",)),
        name="outproj_router",
    )(ao, go, x, woa, wog, g2, wr, br)


def _row(ref, r):
    return ref.at[pl.ds(r, 1), :]


def _dispatch_kernel(pos1_ref, pos2_ref, h_ref, xs_ref, sem):
    tm = h_ref.shape[0]

    def copies(t):
        return [pltpu.make_async_copy(_row(h_ref, t), _row(xs_ref, p[0, t]), sem) for p in (pos1_ref, pos2_ref)]

    def start(t, c):
        for cp in copies(t):
            cp.start()
        return c

    def wait(t, c):
        for cp in copies(t):
            cp.wait()
        return c

    lax.fori_loop(0, tm, start, 0, unroll=MOE_DMA_UNROLL)
    lax.fori_loop(0, tm, wait, 0, unroll=MOE_DMA_UNROLL)


def _dispatch(pos1, pos2, h2, *, tm):
    n = h2.shape[0]
    idx = pl.BlockSpec((None, 1, tm), lambda i: (i, 0, 0), memory_space=pltpu.SMEM)
    return pl.pallas_call(
        _dispatch_kernel,
        grid=(n // tm,),
        in_specs=[idx, idx, pl.BlockSpec((tm, D_MODEL), lambda i: (i, 0))],
        out_specs=pl.BlockSpec(memory_space=pl.ANY),
        out_shape=jax.ShapeDtypeStruct((2 * n, D_MODEL), F32),
        scratch_shapes=[pltpu.SemaphoreType.DMA(())],
        compiler_params=_cparams(("arbitrary",)),
        name="moe_dispatch",
    )(pos1, pos2, h2)


def _experts_kernel(ve_ref, vt_ref, xs_ref, wg_ref, wu_ref, wd_ref, ys_ref, wg_s, wu_s, wd_s):
    v = pl.program_id(0)
    t = xs_ref.shape[0]
    rg = min(MOE_ROW_GROUP, t)

    @pl.when(jnp.logical_or(v == 0, ve_ref[v] != ve_ref[jnp.maximum(v - 1, 0)]))
    def _():
        wg_s[...] = wg_ref[...].astype(BF16)
        wu_s[...] = wu_ref[...].astype(BF16)
        wd_s[...] = wd_ref[...].astype(BF16)

    wg = wg_s[...]
    wu = wu_s[...]
    wd = wd_s[...]
    for g in range(t // rg):
        rows = slice(g * rg, (g + 1) * rg)
        x = xs_ref[rows, :].astype(BF16)
        gate = _dot(x, wg)
        a = ((gate * jax.nn.sigmoid(gate)) * _dot(x, wu)).astype(BF16)
        ys_ref[rows, :] = _dot(a, wd)


def _experts(visit_expert, visit_tile, xs, w_gate, w_up, w_down, *, t):
    nv = visit_expert.shape[0]
    w_in = pl.BlockSpec((None, D_MODEL, D_EXPERT), lambda v, ve, vt: (ve[v], 0, 0))
    w_out = pl.BlockSpec((None, D_EXPERT, D_MODEL), lambda v, ve, vt: (ve[v], 0, 0))
    return pl.pallas_call(
        _experts_kernel,
        grid_spec=pltpu.PrefetchScalarGridSpec(
            num_scalar_prefetch=2, grid=(nv,),
            in_specs=[pl.BlockSpec((t, D_MODEL), lambda v, ve, vt: (vt[v], 0)), w_in, w_in, w_out],
            out_specs=pl.BlockSpec((t, D_MODEL), lambda v, ve, vt: (v, 0)),
            scratch_shapes=[pltpu.VMEM((D_MODEL, D_EXPERT), BF16), pltpu.VMEM((D_MODEL, D_EXPERT), BF16),
                            pltpu.VMEM((D_EXPERT, D_MODEL), BF16)]),
        out_shape=jax.ShapeDtypeStruct((nv * t, D_MODEL), F32),
        compiler_params=_cparams(("arbitrary",)),
        name="moe_experts",
    )(visit_expert, visit_tile, xs, w_gate, w_up, w_down)


def _combine_kernel(pos1_ref, pos2_ref, y1_ref, rect_ref, ys_ref, out_ref, g1_s, g2_s, sem):
    tm = y1_ref.shape[0]

    def copies(t):
        return [pltpu.make_async_copy(_row(ys_ref, p[0, t]), _row(g, t), sem)
                for p, g in ((pos1_ref, g1_s), (pos2_ref, g2_s))]

    def start(t, c):
        for cp in copies(t):
            cp.start()
        return c

    def wait(t, c):
        for cp in copies(t):
            cp.wait()
        return c

    lax.fori_loop(0, tm, start, 0, unroll=MOE_DMA_UNROLL)
    rec = rect_ref[...].T
    w1 = rec[:, R_W1:R_W1 + 1]
    w2 = rec[:, R_W2:R_W2 + 1]
    lax.fori_loop(0, tm, wait, 0, unroll=MOE_DMA_UNROLL)
    out_ref[...] = y1_ref[...] + w1 * g1_s[...] + w2 * g2_s[...]


def _combine(pos1, pos2, y1, rect, ys, *, tm):
    n = y1.shape[0]
    idx = pl.BlockSpec((None, 1, tm), lambda i: (i, 0, 0), memory_space=pltpu.SMEM)
    row = lambda w: pl.BlockSpec((tm, w), lambda i: (i, 0))
    return pl.pallas_call(
        _combine_kernel,
        grid=(n // tm,),
        in_specs=[idx, idx, row(D_MODEL), pl.BlockSpec((R_FIELDS, tm), lambda i: (0, i)),
                  pl.BlockSpec(memory_space=pl.ANY)],
        out_specs=row(D_MODEL),
        out_shape=jax.ShapeDtypeStruct((n, D_MODEL), F32),
        scratch_shapes=[pltpu.VMEM((tm, D_MODEL), F32), pltpu.VMEM((tm, D_MODEL), F32),
                        pltpu.SemaphoreType.DMA(())],
        compiler_params=_cparams(("arbitrary",)),
        name="moe_combine",
    )(pos1, pos2, y1, rect, ys)


def _expert_visits(counts, n_slots, t):
    n_tiles = n_slots // t
    nv = n_tiles + N_EXPERTS - 1
    end = jnp.cumsum(counts)
    start = end - counts
    first_tile = start // t
    last_tile = jnp.maximum(end - 1, start) // t
    n_vis = jnp.where(counts > 0, last_tile - first_tile + 1, 0)
    vis_end = jnp.cumsum(n_vis)
    vis_start = vis_end - n_vis
    v = jnp.arange(nv, dtype=jnp.int32)
    e = jnp.minimum(jnp.sum((vis_end[None, :] <= v[:, None]).astype(jnp.int32), axis=1), N_EXPERTS - 1)
    pick = lambda a: jnp.sum(jnp.where(e[:, None] == jnp.arange(N_EXPERTS)[None, :], a[None, :], 0), axis=1)
    tile = pick(first_tile) + (v - pick(vis_start))
    live = v < vis_end[-1]
    last_e = jnp.max(jnp.where(counts > 0, jnp.arange(N_EXPERTS), 0))
    tile = jnp.where(live, tile, n_tiles - 1)
    e = jnp.where(live, e, last_e)
    return e.astype(jnp.int32), tile.astype(jnp.int32), (vis_start - first_tile).astype(jnp.int32)


def _moe(h2, y1, rect, cnt, w_gate, w_up, w_down):
    n = y1.shape[0]
    tm = min(MOE_TM, n)
    t = min(MOE_SLOT_TILE, 2 * n)
    counts = cnt[ROUTE_OFF:ROUTE_OFF + N_EXPERTS, 0].astype(jnp.int32)
    start = jnp.cumsum(counts) - counts
    visit_expert, visit_tile, tile_shift = _expert_visits(counts, 2 * n, t)
    ids = rect.astype(jnp.int32)
    experts = jnp.arange(N_EXPERTS, dtype=jnp.int32)[:, None]

    def slots(e_row, rank_row):
        mine = ids[e_row][None, :] == experts
        src = jnp.sum(jnp.where(mine, start[:, None], 0), axis=0) + ids[rank_row]
        dst = src + t * jnp.sum(jnp.where(mine, tile_shift[:, None], 0), axis=0)
        return src.reshape(n // tm, 1, tm), dst.reshape(n // tm, 1, tm)

    in1, out1 = slots(R_E1, R_RANK1)
    in2, out2 = slots(R_E2, R_RANK2)
    xs = _dispatch(in1, in2, h2, tm=tm)
    ys = _experts(visit_expert, visit_tile, xs, w_gate, w_up, w_down, t=t)
    return _combine(out1, out2, y1, rect, ys, tm=tm)


def _layer_weights(l, norm1_gain, w_in, a_q_gain, a_k_gain, lambda_q1, lambda_k1, lambda_q2, lambda_k2,
                   a_out_gain, w_a2, b_a, g_out_gain, w_out, norm2_gain, w_group, b_group, w_erouter,
                   b_erouter, w_gate, w_up, w_down):
    w = w_in[l]
    w_la = _fold(w[:, _C_GA:_C_GR], w_a2[l])
    qk_cols = []
    for h in range(G_HEADS):
        qk_cols += [w[:, _C_GQ + h * G_DK:_C_GQ + (h + 1) * G_DK], w[:, _C_GK + h * G_DK:_C_GK + (h + 1) * G_DK]]
    w_big = jnp.concatenate([w[:, :_C_GQ]] + qk_cols + [w[:, _C_GV:_C_GA], w[:, _C_GR:], w_la], axis=1).astype(BF16)
    w_r = jnp.concatenate([w_group[l], w_erouter[l],
                           jnp.zeros((D_MODEL, LANES - N_GROUPS - N_EXPERTS), F32)], axis=1)
    w_rh = w_r.astype(BF16)
    w_r2 = jnp.concatenate([w_rh, (w_r - w_rh.astype(F32)).astype(BF16)], axis=1)
    b_r = jnp.concatenate([b_group[l], b_erouter[l], jnp.zeros((LANES - N_GROUPS - N_EXPERTS,), F32)])[None, :]
    return dict(
        g1=norm1_gain[l][None, :], w_big=w_big,
        qg2=jnp.tile(a_q_gain[l], 2)[None, :], kg2=jnp.tile(a_k_gain[l], 2)[None, :],
        ba=b_a[l][None, :],
        lams=(lambda_q1[l][None, :], lambda_k1[l][None, :], lambda_q2[l][None, :], lambda_k2[l][None, :]),
        a_gain=a_out_gain[l][None, :], g_gain=g_out_gain[l][None, :],
        woa=w_out[l][:A_WIDTH].astype(BF16), wog=w_out[l][A_WIDTH:].astype(BF16),
        g2=norm2_gain[l][None, :], w_r2=w_r2, b_r=b_r,
        w_gate=w_gate[l], w_up=w_up[l], w_down=w_down[l],
    )


def _mix_and_moe(x, p, *, batch, seq, chunk, lam_init, cache=None, state=None):
    prompt = cache is None
    res = _inproj(x, p["g1"], p["w_big"], p["qg2"], p["kg2"], p["ba"], batch=batch, seq=seq, final_layout=prompt)
    if prompt:
        q, kt, ktb, v4, vb, qk, gv, gr, la = res
        ao = _attn_prompt(p["lams"], p["a_gain"], q, ktb, vb, batch=batch, seq=seq, lam_init=lam_init)
        s0 = jnp.zeros((batch, G_HEADS, G_DK, G_DV), F32)
        k_out = jnp.transpose(kt.reshape(batch, A_HEADS, 2, A_DH, seq), (0, 4, 1, 2, 3))
        v_out = v4.reshape(batch, seq, A_HEADS, A_DV)
    else:
        q, k, v, qk, gv, gr, la = res
        kct, vc = cache
        ao = _attn_sample(p["lams"], p["a_gain"], q, k, v, kct, vc, batch=batch, seq=seq,
                          past=kct.shape[-1], lam_init=lam_init)
        s0 = state
        k_out = k.reshape(batch, seq, A_HEADS, 2, A_DH)
        v_out = v.reshape(batch, seq, A_HEADS, A_DV)
    go, st = _gla(qk, la, gv, gr, p["g_gain"], s0, batch=batch, seq=seq, chunk=chunk)
    y1, h2, rect, cnt = _outproj(ao, go, x, p["woa"], p["wog"], p["g2"], p["w_r2"], p["b_r"])
    y = _moe(h2, y1, rect, cnt, p["w_gate"], p["w_up"], p["w_down"])
    return y, k_out, v_out, st


def kernel(x_prompt, x_sample, cache_k, cache_v, state_gla, norm1_gain, w_in, a_q_gain, a_k_gain, lambda_q1, lambda_k1, lambda_q2, lambda_k2, a_out_gain, w_a2, b_a, g_out_gain, w_out, norm2_gain, w_group, b_group, w_erouter, b_erouter, w_gate, w_up, w_down):
    weights = (norm1_gain, w_in, a_q_gain, a_k_gain, lambda_q1, lambda_k1, lambda_q2, lambda_k2, a_out_gain,
               w_a2, b_a, g_out_gain, w_out, norm2_gain, w_group, b_group, w_erouter, b_erouter, w_gate,
               w_up, w_down)
    depth = w_in.shape[0]
    pb, pl_, d = x_prompt.shape
    sb, sl, _ = x_sample.shape
    past = cache_k.shape[2]
    y_p = x_prompt.reshape(pb * pl_, d)
    y_s = x_sample.reshape(sb * sl, d)
    outs = [[] for _ in range(6)]
    for l in range(depth):
        lam_init = 0.8 - 0.6 * math.exp(-0.3 * l)
        p = _layer_weights(l, *weights)
        y_p, kp, vp, sp = _mix_and_moe(y_p, p, batch=pb, seq=pl_, chunk=CHUNK, lam_init=lam_init)
        cache = (jnp.transpose(cache_k[l], (0, 2, 3, 4, 1)).reshape(sb, A_WIDTH, past),
                 cache_v[l].reshape(sb * past * A_HEADS, A_DV))
        y_s, kn, vn, sn = _mix_and_moe(y_s, p, batch=sb, seq=sl, chunk=sl, lam_init=lam_init,
                                       cache=cache, state=state_gla[l])
        for o, t in zip(outs, (kp, vp, sp, kn, vn, sn)):
            o.append(t)
    return (y_p.reshape(pb, pl_, d), y_s.reshape(sb, sl, d)) + tuple(jnp.stack(o) for o in outs)
```
